```python
import math
import jax, jax.numpy as jnp
from jax import lax
import numpy as np

D_MODEL = 2048
BATCH = 1
SEQ = 16384
DEPTH = 1
DEC_BATCH = 32
DEC_SEQ = 8
PAST_LEN = 16384
PAGE_SIZE = 128

D_MIX = D_MODEL
HEAD_DIM = 128
N_HEADS_A = D_MIX // 2 // HEAD_DIM
N_KV_A = N_HEADS_A // 2
N_IDX_HEADS = 16
IDX_DIM = 64
TOPK_MAX = 256
ATT_BLOCK = 128
N_BUCKETS = 32
MAX_DISTANCE = 128
N_HEADS_B = D_MIX // 2 // HEAD_DIM
DK_B = HEAD_DIM
DV_B = HEAD_DIM
CONV_W = 4
CONV_DIM = N_HEADS_B * (2 * DK_B + DV_B)
GDN_CHUNK = 64
N_EXPERTS = 32
TOP_K = 4
D_FF = D_MODEL
SWIGLU_LIMIT = 7.0
SWIGLU_ALPHA = 1.702
MOE_BLOCK = 128
DEEPNORM_ALPHA = (2 * DEPTH) ** 0.25
DEEPNORM_BETA = (8 * DEPTH) ** -0.25
LN_EPS = 1e-5
Q_A = N_HEADS_A * HEAD_DIM
KV_A = N_KV_A * HEAD_DIM
QI_W = N_IDX_HEADS * IDX_DIM
Z_B = N_HEADS_B * DV_B
SPLIT_WIDTHS = (Q_A, KV_A, KV_A, QI_W, IDX_DIM, N_IDX_HEADS, CONV_DIM, Z_B, N_HEADS_B, N_HEADS_B)
D_IN = sum(SPLIT_WIDTHS)

kernel_name = 'hymba_dsa_gdn_moe_step'

F32 = jnp.float32


def layer_norm(x, g, b):
    xf = x.astype(F32)
    mu = jnp.mean(xf, -1, keepdims=True)
    var = jnp.mean(jnp.square(xf - mu), -1, keepdims=True)
    return ((xf - mu) * lax.rsqrt(var + LN_EPS) * g.astype(F32) + b.astype(F32)).astype(x.dtype)


def l2norm(x):
    return x * lax.rsqrt(jnp.sum(x * x, -1, keepdims=True) + 1e-6)


def project(x, w_in):
    B, T, _ = x.shape
    proj = jnp.einsum('btd,de->bte', x, w_in)
    idx = np.cumsum(SPLIT_WIDTHS)[:-1].tolist()
    qa, ka, va, qi, ki, wi, qkv_b, z_b, a_b, b_b = jnp.split(proj, idx, axis=-1)
    return (qa.reshape(B, T, N_HEADS_A, HEAD_DIM), ka.reshape(B, T, N_KV_A, HEAD_DIM),
            va.reshape(B, T, N_KV_A, HEAD_DIM), qi.reshape(B, T, N_IDX_HEADS, IDX_DIM),
            ki, wi, qkv_b, z_b.reshape(B, T, N_HEADS_B, DV_B), a_b, b_b)


def t5_bucket(dist):
    n = jnp.maximum(dist, 0)
    max_exact = N_BUCKETS // 2
    nf = jnp.maximum(n, 1).astype(F32)
    large = max_exact + (jnp.log(nf / max_exact) / math.log(MAX_DISTANCE / max_exact)
                         * (N_BUCKETS - max_exact)).astype(jnp.int32)
    large = jnp.minimum(large, N_BUCKETS - 1)
    return jnp.where(n < max_exact, n, large)


def dsa_block(q, qi, wi, q_pos, k_idx, fetch, topk, rel_table):
    B, T, H, Dh = q.shape
    L = k_idx.shape[1]
    rel = jax.nn.relu(jnp.einsum('bthd,bsd->bths', qi, k_idx).astype(F32) * IDX_DIM ** -0.5)
    score = jnp.einsum('bths,bth->bts', rel, wi.astype(F32) * N_IDX_HEADS ** -0.5)
    causal = jnp.arange(L, dtype=jnp.int32)[None, :] <= q_pos[:, None]
    score = jnp.where(causal[None], score, -jnp.inf)
    _, sel = lax.top_k(score, topk)
    valid = sel <= q_pos[None, :, None]
    k_sel, v_sel = fetch(sel)
    G = H // N_KV_A
    qg = q.reshape(B, T, N_KV_A, G, Dh)
    logits = jnp.einsum('btgrd,btkgd->btgrk', qg, k_sel).astype(F32) * Dh ** -0.5
    bias = rel_table.astype(F32)[t5_bucket(q_pos[None, :, None] - sel)]
    bias = bias.transpose(0, 1, 3, 2).reshape(B, T, N_KV_A, G, topk)
    logits = jnp.where(valid[:, :, None, None, :], logits + bias, -jnp.inf)
    p = jax.nn.softmax(logits, axis=-1).astype(v_sel.dtype)
    out = jnp.einsum('btgrk,btkgd->btgrd', p, v_sel)
    return out.reshape(B, T, H * Dh)


def take_rows(a, s):
    return jax.vmap(lambda ab, sb: ab[sb])(a, s)


def prompt_attention(qa, ka, va, qi, ki, wi, rel_table):
    B, T = qa.shape[:2]
    topk = min(TOPK_MAX, T // 4)
    nb = T // ATT_BLOCK

    def fetch(sel):
        return take_rows(ka, sel), take_rows(va, sel)

    def blk(i):
        s = i * ATT_BLOCK
        sl = lambda a: lax.dynamic_slice_in_dim(a, s, ATT_BLOCK, axis=1)
        q_pos = s + jnp.arange(ATT_BLOCK, dtype=jnp.int32)
        return dsa_block(sl(qa), sl(qi), sl(wi), q_pos, ki, fetch, topk, rel_table)

    out = lax.map(blk, jnp.arange(nb, dtype=jnp.int32))
    return out.transpose(1, 0, 2, 3).reshape(B, T, -1)


def sample_attention(qa, ka, va, qi, ki, wi, cache_k, cache_v, cache_kidx, page_table, rel_table):
    DB, T = qa.shape[:2]
    n_pages = page_table.shape[1]
    past_kidx = cache_kidx[page_table].reshape(DB, n_pages * PAGE_SIZE, IDX_DIM).astype(ki.dtype)
    k_idx = jnp.concatenate([past_kidx, ki], axis=1)
    topk = min(TOPK_MAX, (PAST_LEN + T) // 4)
    q_pos = PAST_LEN + jnp.arange(T, dtype=jnp.int32)

    def fetch(sel):
        in_past = (sel < PAST_LEN)[..., None, None]
        ps = jnp.minimum(sel, PAST_LEN - 1)
        phys = take_rows(page_table, ps // PAGE_SIZE)
        off = ps % PAGE_SIZE
        ns = jnp.clip(sel - PAST_LEN, 0, T - 1)
        k_sel = jnp.where(in_past, cache_k[phys, off].astype(ka.dtype), take_rows(ka, ns))
        v_sel = jnp.where(in_past, cache_v[phys, off].astype(va.dtype), take_rows(va, ns))
        return k_sel, v_sel

    return dsa_block(qa, qi, wi, q_pos, k_idx, fetch, topk, rel_table)


def chunk_gated_delta(q, k, v, g, beta, s0):
    B, T, H, DK = q.shape
    DV = v.shape[-1]
    C = math.gcd(T, GDN_CHUNK)
    N = T // C

    def chunks(t):
        return jnp.moveaxis(t.reshape((B, N, C, H) + t.shape[3:]), 3, 1)

    q, k, v, g, beta = (chunks(t) for t in (q, k, v, g, beta))
    gc = jnp.cumsum(g, axis=-1)
    incl = jnp.tril(jnp.ones((C, C), bool))
    strict = jnp.tril(jnp.ones((C, C), bool), -1)
    diff = gc[..., :, None] - gc[..., None, :]
    decay = jnp.where(incl, jnp.exp(jnp.where(incl, diff, 0.0)), 0.0)
    kk = jnp.einsum('bhnid,bhnjd->bhnij', k, k)
    tmat = jnp.eye(C, dtype=F32) + jnp.where(strict, beta[..., :, None] * decay * kk, 0.0)
    u0 = lax.linalg.triangular_solve(tmat, v * beta[..., None], left_side=True, lower=True, unit_diagonal=True)
    w = lax.linalg.triangular_solve(tmat, k * (beta * jnp.exp(gc))[..., None], left_side=True, lower=True,
                                    unit_diagonal=True)
    qk = jnp.einsum('bhnid,bhnjd->bhnij', q, k) * decay
    q_dec = q * jnp.exp(gc)[..., None]
    k_dec = k * jnp.exp(gc[..., -1:] - gc)[..., None]
    g_end = jnp.exp(gc[..., -1])

    def step(s, inp):
        u0_n, w_n, qk_n, qd_n, kd_n, ge_n = inp
        u = u0_n - jnp.einsum('bhck,bhkv->bhcv', w_n, s)
        o = jnp.einsum('bhck,bhkv->bhcv', qd_n, s) + jnp.einsum('bhij,bhjv->bhiv', qk_n, u)
        s = ge_n[..., None, None] * s + jnp.einsum('bhck,bhcv->bhkv', kd_n, u)
        return s, o

    xs = tuple(jnp.moveaxis(t, 2, 0) for t in (u0, w, qk, q_dec, k_dec, g_end))
    s_final, o = lax.scan(step, s0, xs)
    o = o.transpose(1, 0, 3, 2, 4).reshape(B, T, H, DV)
    return o, s_final


def gdn_mixer(qkv, z, a, b, conv_buf, s0, conv_w, A_log, dt_bias, norm_w):
    B, T, _ = qkv.shape
    xp = jnp.concatenate([conv_buf.astype(qkv.dtype), qkv], axis=1)
    conv = xp[:, 0:T] * conv_w[0]
    for i in range(1, CONV_W):
        conv = conv + xp[:, i:i + T] * conv_w[i]
    new_buf = xp[:, T:]
    h = jax.nn.silu(conv.astype(F32))
    q, k, v = jnp.split(h, [N_HEADS_B * DK_B, 2 * N_HEADS_B * DK_B], axis=-1)
    q = l2norm(q.reshape(B, T, N_HEADS_B, DK_B)) * DK_B ** -0.5
    k = l2norm(k.reshape(B, T, N_HEADS_B, DK_B))
    v = v.reshape(B, T, N_HEADS_B, DV_B)
    g = -jnp.exp(A_log.astype(F32)) * jax.nn.softplus(a.astype(F32) + dt_bias.astype(F32))
    beta = jax.nn.sigmoid(b.astype(F32))
    o, s_new = chunk_gated_delta(q, k, v, g, beta, s0.astype(F32))
    o = o * lax.rsqrt(jnp.mean(o * o, -1, keepdims=True) + 1e-6) * norm_w.astype(F32) * jax.nn.silu(z.astype(F32))
    return o.reshape(B, T, -1).astype(qkv.dtype), new_buf, s_new


def moe_ffn(h, w_router, b_router, w1, b1, w2, b2):
    n, d = h.shape
    logits = jnp.einsum('nd,de->ne', h, w_router).astype(F32) + b_router.astype(F32)
    top_logit, top_e = lax.top_k(logits, TOP_K)
    gate = jax.nn.softmax(top_logit, axis=-1)
    nk = n * TOP_K
    n_blocks = (nk + N_EXPERTS * (MOE_BLOCK - 1) + MOE_BLOCK - 1) // MOE_BLOCK
    e_flat = top_e.reshape(nk)
    tok_flat = jnp.arange(nk, dtype=jnp.int32) // TOP_K
    gate_flat = gate.reshape(nk)
    order = jnp.argsort(e_flat)
    e_sorted = e_flat[order]
    counts = jnp.zeros((N_EXPERTS,), jnp.int32).at[e_flat].add(1)
    padded = (counts + MOE_BLOCK - 1) // MOE_BLOCK * MOE_BLOCK
    start = jnp.cumsum(counts) - counts
    pend = jnp.cumsum(padded)
    pstart = pend - padded
    dest = pstart[e_sorted] + jnp.arange(nk, dtype=jnp.int32) - start[e_sorted]
    row_tok = jnp.full((n_blocks * MOE_BLOCK,), n, jnp.int32).at[dest].set(tok_flat[order])
    row_gate = jnp.zeros((n_blocks * MOE_BLOCK,), F32).at[dest].set(gate_flat[order])
    blk_e = jnp.minimum(jnp.searchsorted(pend, jnp.arange(n_blocks, dtype=jnp.int32) * MOE_BLOCK, side='right'),
                        N_EXPERTS - 1)
    x_rows = jnp.concatenate([h, jnp.zeros((1, d), h.dtype)], axis=0)[row_tok].reshape(n_blocks, MOE_BLOCK, d)

    def expert_block(args):
        xb, e = args
        hh = xb @ w1[e] + b1[e]
        glu = jnp.minimum(hh[:, :D_FF], SWIGLU_LIMIT)
        lin = jnp.clip(hh[:, D_FF:], -SWIGLU_LIMIT, SWIGLU_LIMIT)
        act = glu * jax.nn.sigmoid(SWIGLU_ALPHA * glu) * (lin + 1.0)
        return act @ w2[e] + b2[e]

    y_rows = lax.map(expert_block, (x_rows, blk_e)).reshape(-1, d).astype(F32) * row_gate[:, None]
    y = jnp.zeros((n + 1, d), F32).at[row_tok].add(y_rows)
    return y[:n].astype(h.dtype)


def finish_layer(x, o_a, o_b, w_out, ln1_g, ln1_b, w_router, b_router, w1, b1, w2, b2, ln2_g, ln2_b):
    B, T, D = x.shape
    mix = jnp.einsum('btm,md->btd', jnp.concatenate([o_a, o_b], axis=-1), w_out)
    h = layer_norm(DEEPNORM_ALPHA * x + mix, ln1_g, ln1_b)
    f = moe_ffn(h.reshape(B * T, D), w_router, b_router, w1, b1, w2, b2).reshape(B, T, D)
    return layer_norm(DEEPNORM_ALPHA * h + f, ln2_g, ln2_b)


def setup_inputs(seed: int = 0) -> dict:
    key = jax.random.key(seed)
    ks = jax.random.split(key, 26)
    nrm = lambda k, shape, s: jax.random.normal(k, shape, F32) * s
    n_pages = PAST_LEN // PAGE_SIZE
    n_used = DEC_BATCH * n_pages
    n_pool = n_used + max(1, n_used // 4)
    page_table = jax.random.permutation(ks[0], n_pool)[:n_used].reshape(DEC_BATCH, n_pages).astype(jnp.int32)
    segs = [(Q_A, 1.0), (KV_A, 1.0), (KV_A, DEEPNORM_BETA), (QI_W, 1.0), (IDX_DIM, 1.0), (N_IDX_HEADS, 1.0),
            (2 * N_HEADS_B * DK_B, 1.0), (N_HEADS_B * DV_B, DEEPNORM_BETA), (Z_B, 1.0), (N_HEADS_B, 1.0),
            (N_HEADS_B, 1.0)]
    col_scale = jnp.concatenate([jnp.full((w,), s, F32) for w, s in segs])
    w_in = nrm(ks[1], (D_MODEL, D_IN), D_MODEL ** -0.5) * col_scale[None, :]
    dt = jnp.exp(jax.random.uniform(ks[2], (N_HEADS_B,), F32, math.log(1e-3), math.log(1e-1)))
    return {
        'x_prompt': nrm(ks[3], (BATCH, SEQ, D_MODEL), 1.0),
        'x_sample': nrm(ks[4], (DEC_BATCH, DEC_SEQ, D_MODEL), 1.0),
        'cache_k': nrm(ks[5], (n_pool, PAGE_SIZE, N_KV_A, HEAD_DIM), 1.0),
        'cache_v': nrm(ks[6], (n_pool, PAGE_SIZE, N_KV_A, HEAD_DIM), DEEPNORM_BETA),
        'cache_kidx': nrm(ks[7], (n_pool, PAGE_SIZE, IDX_DIM), 1.0),
        'state_conv': nrm(ks[8], (DEC_BATCH, CONV_W - 1, CONV_DIM), 1.0),
        'state_ssm': nrm(ks[9], (DEC_BATCH, N_HEADS_B, DK_B, DV_B), 0.1),
        'page_table': page_table,
        'w_in': w_in,
        'rel_table': nrm(ks[10], (N_BUCKETS, N_HEADS_A), 0.5),
        'w_out': nrm(ks[11], (D_MIX, D_MODEL), D_MIX ** -0.5 * DEEPNORM_BETA),
        'conv_w': nrm(ks[12], (CONV_W, CONV_DIM), CONV_W ** -0.5),
        'A_log': jnp.log(jax.random.uniform(ks[13], (N_HEADS_B,), F32, 1.0, 16.0)),
        'dt_bias': dt + jnp.log(-jnp.expm1(-dt)),
        'gdn_norm_w': 1.0 + nrm(ks[14], (DV_B,), 0.05),
        'ln1_g': 1.0 + nrm(ks[15], (D_MODEL,), 0.05),
        'ln1_b': nrm(ks[16], (D_MODEL,), 0.02),
        'w_router': nrm(ks[17], (D_MODEL, N_EXPERTS), D_MODEL ** -0.5),
        'b_router': nrm(ks[18], (N_EXPERTS,), 0.01),
        'w1': nrm(ks[19], (N_EXPERTS, D_MODEL, 2 * D_FF), D_MODEL ** -0.5),
        'b1': nrm(ks[20], (N_EXPERTS, 2 * D_FF), 0.01),
        'w2': nrm(ks[21], (N_EXPERTS, D_FF, D_MODEL), D_FF ** -0.5 * DEEPNORM_BETA),
        'b2': nrm(ks[22], (N_EXPERTS, D_MODEL), 0.01),
        'ln2_g': 1.0 + nrm(ks[23], (D_MODEL,), 0.05),
        'ln2_b': nrm(ks[24], (D_MODEL,), 0.02),
    }


def reference(x_prompt, x_sample, cache_k, cache_v, cache_kidx, state_conv, state_ssm, page_table,
              w_in, rel_table, w_out, conv_w, A_log, dt_bias, gdn_norm_w, ln1_g, ln1_b,
              w_router, b_router, w1, b1, w2, b2, ln2_g, ln2_b):
    B = x_prompt.shape[0]
    qa, ka, va, qi, ki, wi, qkv_b, z_b, a_b, b_b = project(x_prompt, w_in)
    o_a = prompt_attention(qa, ka, va, qi, ki, wi, rel_table)
    conv0 = jnp.zeros((B, CONV_W - 1, CONV_DIM), x_prompt.dtype)
    ssm0 = jnp.zeros((B, N_HEADS_B, DK_B, DV_B), F32)
    o_b, conv_prompt, ssm_prompt = gdn_mixer(qkv_b, z_b, a_b, b_b, conv0, ssm0, conv_w, A_log, dt_bias, gdn_norm_w)
    y_prompt = finish_layer(x_prompt, o_a, o_b, w_out, ln1_g, ln1_b, w_router, b_router, w1, b1, w2, b2,
                            ln2_g, ln2_b)
    k_prompt, v_prompt, kidx_prompt = ka, va, ki
    qa, ka, va, qi, ki, wi, qkv_b, z_b, a_b, b_b = project(x_sample, w_in)
    o_a = sample_attention(qa, ka, va, qi, ki, wi, cache_k, cache_v, cache_kidx, page_table, rel_table)
    o_b, conv_sample, ssm_sample = gdn_mixer(qkv_b, z_b, a_b, b_b, state_conv, state_ssm, conv_w, A_log, dt_bias,
                                             gdn_norm_w)
    y_sample = finish_layer(x_sample, o_a, o_b, w_out, ln1_g, ln1_b, w_router, b_router, w1, b1, w2, b2,
                            ln2_g, ln2_b)
    k_sample, v_sample, kidx_sample = ka, va, ki
    return (y_prompt, y_sample, k_prompt, v_prompt, kidx_prompt, conv_prompt, ssm_prompt,
            k_sample, v_sample, kidx_sample, conv_sample, ssm_sample)
```

```python
import functools
import math

import jax
import jax.numpy as jnp
import numpy as np
from jax import lax
from jax.experimental import pallas as pl
from jax.experimental.pallas import tpu as pltpu

F32 = jnp.float32
BF16 = jnp.bfloat16
I32 = jnp.int32

D_MODEL = 2048
HEAD_DIM = 128
N_HEADS_A = 8
N_KV_A = 4
N_IDX_HEADS = 16
IDX_DIM = 64
TOPK_MAX = 256
N_BUCKETS = 32
MAX_DISTANCE = 128
N_HEADS_B = 8
CONV_W = 4
CONV_DIM = N_HEADS_B * 3 * HEAD_DIM
TOP_K = 4
D_FF = D_MODEL
SWIGLU_LIMIT = 7.0
SWIGLU_ALPHA = 1.702
PAGE_SIZE = 128
DEPTH = 1
DEEPNORM_ALPHA = (2 * DEPTH) ** 0.25
LN_EPS = 1e-5

Q_A = N_HEADS_A * HEAD_DIM
KV_A = N_KV_A * HEAD_DIM
QI_W = N_IDX_HEADS * IDX_DIM
Z_B = N_HEADS_B * HEAD_DIM

LANE = 128
SUBLANE = 8
VMEM_LIMIT = 56 * 1024 * 1024

C_QA = 0
C_KA = C_QA + Q_A
C_VA = C_KA + KV_A
C_QI = C_VA + KV_A
C_QKV = C_QI + QI_W
C_Z = C_QKV + CONV_DIM
C_MISC = C_Z + Z_B
M_KI, M_WI, M_A, M_B = 0, 64, 80, 88
N_PROJ = 7680

NEG = -1e30
GDN_C = 128


def _cparams(sem, vmem=VMEM_LIMIT):
    return pltpu.CompilerParams(dimension_semantics=sem, vmem_limit_bytes=vmem)


def _proj_body(x_ref, w_ref, o_ref, xb_ref):
    @pl.when(pl.program_id(1) == 0)
    def _():
        xb_ref[...] = x_ref[...].astype(BF16)

    o_ref[...] = jnp.dot(xb_ref[...], w_ref[...], preferred_element_type=F32)


def _proj(x, w, tm, tn):
    T, D = x.shape
    N = w.shape[1]
    return pl.pallas_call(
        _proj_body,
        grid=(T // tm, N // tn),
        in_specs=[pl.BlockSpec((tm, D), lambda i, j: (i, 0)),
                  pl.BlockSpec((D, tn), lambda i, j: (0, j))],
        out_specs=pl.BlockSpec((tm, tn), lambda i, j: (i, j)),
        out_shape=jax.ShapeDtypeStruct((T, N), F32),
        scratch_shapes=[pltpu.VMEM((tm, D), BF16)],
        compiler_params=_cparams(("arbitrary", "arbitrary")),
        name="in_proj",
    )(x, w)


def _arrange_w_in(w_in):
    offs = np.cumsum([0, Q_A, KV_A, KV_A, QI_W, IDX_DIM, N_IDX_HEADS, CONV_DIM, Z_B, N_HEADS_B, N_HEADS_B])
    seg = lambda k: w_in[:, offs[k]:offs[k + 1]]
    qa, ka, va, qi, ki, wi, qkv, z, a, b = (seg(k) for k in range(10))
    pad_misc = jnp.zeros((w_in.shape[0], LANE - (IDX_DIM + N_IDX_HEADS + 2 * N_HEADS_B)), w_in.dtype)
    pad_tail = jnp.zeros((w_in.shape[0], N_PROJ - (C_MISC + LANE)), w_in.dtype)
    return jnp.concatenate([qa, ka, va, qi, qkv, z, ki, wi, a, b, pad_misc, pad_tail], axis=1).astype(BF16)


def _bias_body(rel_ref, o_ref):
    d = pl.program_id(0)
    s = lax.broadcasted_iota(I32, (LANE, LANE), 0)
    t = lax.broadcasted_iota(I32, (LANE, LANE), 1)
    n = jnp.maximum(d * LANE + t - s, 0)
    max_exact = N_BUCKETS // 2
    nf = jnp.maximum(n, 1).astype(F32)
    large = max_exact + (jnp.log(nf / max_exact) / math.log(MAX_DISTANCE / max_exact)
                         * (N_BUCKETS - max_exact)).astype(I32)
    large = jnp.minimum(large, N_BUCKETS - 1)
    bucket = jnp.where(n < max_exact, n, large)
    for h in range(N_HEADS_A):
        acc = jnp.zeros((LANE, LANE), F32)
        for b in range(N_BUCKETS):
            acc = jnp.where(bucket == b, rel_ref[b, h], acc)
        o_ref[0, h] = acc


def _bias_tiles(rel_table):
    return pl.pallas_call(
        _bias_body,
        grid=(2,),
        in_specs=[pl.BlockSpec(memory_space=pltpu.SMEM)],
        out_specs=pl.BlockSpec((1, N_HEADS_A, LANE, LANE), lambda d: (d, 0, 0, 0)),
        out_shape=jax.ShapeDtypeStruct((2, N_HEADS_A, LANE, LANE), F32),
        compiler_params=_cparams(("arbitrary",)),
        name="rel_bias_tiles",
    )(rel_table.astype(F32))


INT_MIN = -2 ** 31
NEG_INF_KEY = int(np.int32(np.array(-np.inf, np.float32).view(np.int32)) ^ np.int32(0x7FFFFFFF))


def _sortable(x):
    b = pltpu.bitcast(x, I32)
    return b ^ (lax.shift_right_arithmetic(b, 31) & jnp.int32(0x7FFFFFFF))


def _kth_largest_key(count_ge, shape, k):
    def bit_body(b, c):
        cand = c + lax.shift_left(jnp.int32(1), 31 - b)
        return jnp.where(count_ge(cand) >= k, cand, c)

    return lax.fori_loop(0, 32, bit_body, jnp.full(shape, INT_MIN, I32))


IDX_CH = 512


def _pidx_body(qi_ref, w_ref, ki_ref, o_ref, keys_ref, *, topk, n_keys):
    i = pl.program_id(0)
    nch = (i * LANE + LANE + IDX_CH - 1) // IDX_CH
    t_idx = i * LANE + lax.broadcasted_iota(I32, (IDX_CH, LANE), 1)
    s_loc = lax.broadcasted_iota(I32, (IDX_CH, LANE), 0)

    def score_chunk(c, carry):
        k0 = pl.multiple_of(c * IDX_CH, IDX_CH)
        kc = ki_ref[pl.ds(k0, IDX_CH), :]
        acc = jnp.zeros((IDX_CH, LANE), F32)
        for p in range(N_IDX_HEADS // 2):
            r = jnp.dot(kc, qi_ref[0, p], preferred_element_type=F32)
            acc = acc + w_ref[0, 2 * p:2 * p + 1, :] * jnp.maximum(r[:, :LANE], 0.0)
            acc = acc + w_ref[0, 2 * p + 1:2 * p + 2, :] * jnp.maximum(r[:, LANE:], 0.0)
        acc = jnp.where(k0 + s_loc <= t_idx, acc, -jnp.inf)
        keys_ref[pl.ds(k0, IDX_CH), :] = _sortable(acc)
        return carry

    lax.fori_loop(0, nch, score_chunk, 0)

    def count(pred):
        def body(c, cnt):
            k0 = pl.multiple_of(c * IDX_CH, IDX_CH)
            m = pred(keys_ref[pl.ds(k0, IDX_CH), :], k0).astype(I32)
            return cnt + jnp.sum(m.reshape(IDX_CH // SUBLANE, SUBLANE, LANE), axis=0)

        cnt = lax.fori_loop(0, nch, body, jnp.zeros((SUBLANE, LANE), I32))
        return jnp.sum(cnt, axis=0, keepdims=True)

    thr = _kth_largest_key(lambda c: count(lambda kb, k0: kb >= c), (1, LANE), topk)
    n_ge = count(lambda kb, k0: kb >= thr)
    n_gt = count(lambda kb, k0: kb > thr)
    need = jnp.where(thr == NEG_INF_KEY, 0, topk - n_gt)
    excess = jnp.max(n_ge - n_gt - need) > 0

    def tie_limit():
        nbits = max(1, int(math.ceil(math.log2(n_keys))))
        def bit_body(b, p):
            cand = p + lax.shift_left(jnp.int32(1), nbits - 1 - b)
            f = count(lambda kb, k0: (kb == thr) & (k0 + s_loc < cand))
            return jnp.where(f < need, cand, p)
        return lax.fori_loop(0, nbits, bit_body, jnp.zeros((1, LANE), I32))

    lim = lax.cond(excess, tie_limit, lambda: jnp.full((1, LANE), n_keys, I32))
    lim = jnp.where(need > 0, lim, -1)

    def write_chunk(c, carry):
        k0 = pl.multiple_of(c * IDX_CH, IDX_CH)
        kb = keys_ref[pl.ds(k0, IDX_CH), :]
        sel = (kb > thr) | ((kb == thr) & (k0 + s_loc <= lim))
        o_ref[pl.ds(k0, IDX_CH), :] = jnp.where(sel, 0.0, NEG).astype(o_ref.dtype)
        return carry

    lax.fori_loop(0, nch, write_chunk, 0)

    def fill_chunk(c, carry):
        k0 = pl.multiple_of(c * IDX_CH, IDX_CH)
        o_ref[pl.ds(k0, IDX_CH), :] = jnp.full((IDX_CH, LANE), NEG, o_ref.dtype)
        return carry

    lax.fori_loop(nch, n_keys // IDX_CH, fill_chunk, 0)


def _prompt_indexer(qi_t, w_t, ki, topk):
    nq = qi_t.shape[0]
    T = ki.shape[0]
    return pl.pallas_call(
        functools.partial(_pidx_body, topk=topk, n_keys=T),
        grid=(nq,),
        in_specs=[pl.BlockSpec((1, N_IDX_HEADS // 2, IDX_DIM, 2 * LANE), lambda i: (i, 0, 0, 0)),
                  pl.BlockSpec((1, N_IDX_HEADS, LANE), lambda i: (i, 0, 0)),
                  pl.BlockSpec((T, IDX_DIM), lambda i: (0, 0))],
        out_specs=pl.BlockSpec((T, LANE), lambda i: (0, i)),
        out_shape=jax.ShapeDtypeStruct((T, T), BF16),
        scratch_shapes=[pltpu.VMEM((T, LANE), I32)],
        compiler_params=_cparams(("arbitrary",)),
        name="prompt_indexer",
    )(qi_t, w_t, ki)


ATT_TK = 512


def _pattn_body(ii_ref, jj_ref, c31_ref, q_ref, k_ref, vt_ref, mask_ref, bias_ref, o_ref, m_ref, l_ref, acc_ref):
    s = pl.program_id(0)
    i = ii_ref[s]
    j = jj_ref[s]
    G = N_HEADS_A // N_KV_A
    W = G * LANE

    @pl.when(j == 0)
    def _():
        m_ref[...] = jnp.full(m_ref.shape, NEG, F32)
        l_ref[...] = jnp.zeros(l_ref.shape, F32)
        acc_ref[...] = jnp.zeros(acc_ref.shape, F32)

    for u in range(ATT_TK // LANE):
        delta = i * LANE - j * ATT_TK - u * LANE
        rows = slice(u * LANE, (u + 1) * LANE)

        @pl.when(delta >= 0)
        def _():
            mb = mask_ref[rows, :].astype(F32)
            for g in range(N_KV_A):
                kg = k_ref[rows, g * HEAD_DIM:(g + 1) * HEAD_DIM]
                st = jnp.dot(kg, q_ref[0, g], preferred_element_type=F32)
                parts = []
                for r in range(G):
                    h = g * G + r
                    bt = jnp.where(delta == 0, bias_ref[0, h],
                                   jnp.where(delta == LANE, bias_ref[1, h], c31_ref[h]))
                    parts.append(st[:, r * LANE:(r + 1) * LANE] + bt + mb)
                lg = jnp.concatenate(parts, axis=1)
                m_old = m_ref[g]
                m_new = jnp.maximum(m_old, jnp.max(lg, axis=0, keepdims=True))
                alpha = jnp.exp(m_old - m_new)
                p = jnp.exp(lg - m_new)
                l_ref[g] = alpha * l_ref[g] + jnp.sum(p, axis=0, keepdims=True)
                m_ref[g] = m_new
                vg = vt_ref[g * HEAD_DIM:(g + 1) * HEAD_DIM, rows]
                acc_ref[g] = alpha * acc_ref[g] + jnp.dot(vg, p.astype(BF16), preferred_element_type=F32)

    @pl.when((j + 1) * ATT_TK > i * LANE + LANE - 1)
    def _():
        for g in range(N_KV_A):
            o_ref[0, g] = acc_ref[g] / l_ref[g]


def _prompt_attention(q_t, k, v_t, mask, bias, c31):
    nq = q_t.shape[0]
    T = k.shape[0]
    ii, jj = [], []
    for i in range(nq):
        for j in range((i * LANE + LANE - 1) // ATT_TK + 1):
            ii.append(i)
            jj.append(j)
    ii = jnp.asarray(np.array(ii, np.int32))
    jj = jnp.asarray(np.array(jj, np.int32))
    G = N_HEADS_A // N_KV_A
    grid_spec = pltpu.PrefetchScalarGridSpec(
        num_scalar_prefetch=3,
        grid=(int(ii.shape[0]),),
        in_specs=[pl.BlockSpec((1, N_KV_A, HEAD_DIM, G * LANE), lambda s, ii, jj, c: (ii[s], 0, 0, 0)),
                  pl.BlockSpec((ATT_TK, KV_A), lambda s, ii, jj, c: (jj[s], 0)),
                  pl.BlockSpec((KV_A, ATT_TK), lambda s, ii, jj, c: (0, jj[s])),
                  pl.BlockSpec((ATT_TK, LANE), lambda s, ii, jj, c: (jj[s], ii[s])),
                  pl.BlockSpec((2, N_HEADS_A, LANE, LANE), lambda s, ii, jj, c: (0, 0, 0, 0))],
        out_specs=pl.BlockSpec((1, N_KV_A, HEAD_DIM, G * LANE), lambda s, ii, jj, c: (ii[s], 0, 0, 0)),
        scratch_shapes=[pltpu.VMEM((N_KV_A, 1, G * LANE), F32),
                        pltpu.VMEM((N_KV_A, 1, G * LANE), F32),
                        pltpu.VMEM((N_KV_A, HEAD_DIM, G * LANE), F32)],
    )
    return pl.pallas_call(
        _pattn_body,
        grid_spec=grid_spec,
        out_shape=jax.ShapeDtypeStruct((nq, N_KV_A, HEAD_DIM, G * LANE), F32),
        compiler_params=_cparams(("arbitrary",)),
        name="prompt_attention",
    )(ii, jj, c31, q_t, k, v_t, mask, bias)


def _split3(a):
    a1 = a.astype(BF16)
    r = a - a1.astype(F32)
    a2 = r.astype(BF16)
    a3 = (r - a2.astype(F32)).astype(BF16)
    return a1, a2, a3


def _dot_f32(a, b):
    a1, a2, a3 = _split3(a)
    b1, b2, b3 = _split3(b)
    d = lambda p, q: jnp.dot(p, q, preferred_element_type=F32)
    return (d(a1, b1) + (d(a1, b2) + d(a2, b1))) + ((d(a1, b3) + d(a3, b1)) + d(a2, b2))


def _dot_bf(a, b):
    return jnp.dot(a.astype(BF16), b.astype(BF16), preferred_element_type=F32)


def _dot_nt(a, b):
    return lax.dot_general(a.astype(BF16), b.astype(BF16), (((1,), (1,)), ((), ())), preferred_element_type=F32)


def _sigmoid(x):
    return 1.0 / (1.0 + jnp.exp(-x))


def _softplus(x):
    return jnp.maximum(x, 0.0) + jnp.log(1.0 + jnp.exp(-jnp.abs(x)))


def _gdn_body(qkv_ref, z_ref, misc_ref, conv0_ref, s0_ref, cw_ref, arow_ref, dtrow_ref, nw_ref,
              o_ref, convo_ref, so_ref, xbuf_ref, s_ref, *, n_valid):
    C = GDN_C
    n = pl.program_id(1)
    H = N_HEADS_B

    @pl.when(n == 0)
    def _():
        xbuf_ref[0:SUBLANE, :] = conv0_ref[0]
        s_ref[...] = s0_ref[0]
        if n_valid < C:
            xbuf_ref[SUBLANE:, :] = jnp.zeros((C, CONV_DIM), F32)

    xbuf_ref[SUBLANE:SUBLANE + n_valid, :] = qkv_ref[...]
    conv = xbuf_ref[5:5 + C, :] * cw_ref[0:1, :]
    for i in range(1, CONV_W):
        conv = conv + xbuf_ref[5 + i:5 + i + C, :] * cw_ref[i:i + 1, :]
    tail = xbuf_ref[n_valid:n_valid + SUBLANE, :]
    xbuf_ref[0:SUBLANE, :] = tail
    convo_ref[0] = tail

    row = lax.broadcasted_iota(I32, (C, LANE), 0)
    col = lax.broadcasted_iota(I32, (C, LANE), 1)
    live = row < n_valid
    hact = conv * _sigmoid(conv)

    misc = misc_ref[...]
    if n_valid < C:
        misc = jnp.concatenate([misc, jnp.zeros((C - n_valid, LANE), F32)], axis=0)
    g_full = jnp.where(live, -jnp.exp(arow_ref[...]) * _softplus(misc + dtrow_ref[...]), 0.0)
    beta_full = jnp.where(live, _sigmoid(misc), 0.0)
    tril = (row >= col).astype(F32)
    gc_full = _dot_f32(tril, g_full)
    gc_t = gc_full.T
    strict = row > col
    incl = row >= col

    for hd in range(H):
        sl = slice(hd * HEAD_DIM, (hd + 1) * HEAD_DIM)
        q = hact[:, sl]
        k = hact[:, H * HEAD_DIM + hd * HEAD_DIM:H * HEAD_DIM + (hd + 1) * HEAD_DIM]
        v = hact[:, 2 * H * HEAD_DIM + hd * HEAD_DIM:2 * H * HEAD_DIM + (hd + 1) * HEAD_DIM]
        q = q * lax.rsqrt(jnp.sum(q * q, -1, keepdims=True) + 1e-6) * HEAD_DIM ** -0.5
        k = k * lax.rsqrt(jnp.sum(k * k, -1, keepdims=True) + 1e-6)
        q = jnp.where(live, q, 0.0)
        k = jnp.where(live, k, 0.0)
        gc_col = jnp.broadcast_to(gc_full[:, M_A + hd:M_A + hd + 1], (C, LANE))
        gc_row = jnp.broadcast_to(gc_t[M_A + hd:M_A + hd + 1, :], (C, LANE))
        gc_last = jnp.broadcast_to(gc_full[C - 1:C, M_A + hd:M_A + hd + 1], (C, LANE))
        beta = jnp.broadcast_to(beta_full[:, M_B + hd:M_B + hd + 1], (C, LANE))
        decay = jnp.where(incl, jnp.exp(jnp.where(incl, gc_col - gc_row, 0.0)), 0.0)
        kk = _dot_nt(k, k)
        qk = _dot_nt(q, k) * decay
        e_gc = jnp.exp(gc_col)
        m = jnp.where(strict, -(beta * decay * kk), 0.0)
        y = jnp.concatenate([v * beta, k * (beta * e_gc)], axis=1)
        nlev = int(math.log2(C))
        for lvl in range(nlev):
            if lvl + 1 < nlev:
                mp = _dot_f32(m, jnp.concatenate([m, y], axis=1))
                y = y + mp[:, C:]
                m = mp[:, :C]
            else:
                y = y + _dot_f32(m, y)
        u0 = y[:, :HEAD_DIM]
        w = y[:, HEAD_DIM:]
        s_old = s_ref[hd]
        u = u0 - _dot_bf(w, s_old)
        o = _dot_bf(q * e_gc, s_old) + _dot_bf(qk, u)
        kd = k * jnp.exp(gc_last - gc_col)
        s_ref[hd] = jnp.exp(gc_last) * s_old + _dot_bf(kd.T, u)
        o = o * lax.rsqrt(jnp.mean(o * o, -1, keepdims=True) + 1e-6) * nw_ref[...]
        zz = z_ref[:, sl]
        o_ref[:, sl] = o[:n_valid] * (zz * _sigmoid(zz))

    so_ref[0] = s_ref[...]


def _gdn(proj, row0, B, T, conv0, s0, conv_w, a_row, dt_row, norm_w):
    n_valid = min(T, GDN_C)
    nblk = T // n_valid
    r0 = row0 // n_valid
    rb = lambda b, n: r0 + b * nblk + n
    return pl.pallas_call(
        functools.partial(_gdn_body, n_valid=n_valid),
        grid=(B, nblk),
        in_specs=[pl.BlockSpec((n_valid, CONV_DIM), lambda b, n: (rb(b, n), C_QKV // CONV_DIM)),
                  pl.BlockSpec((n_valid, Z_B), lambda b, n: (rb(b, n), C_Z // Z_B)),
                  pl.BlockSpec((n_valid, LANE), lambda b, n: (rb(b, n), C_MISC // LANE)),
                  pl.BlockSpec((1, SUBLANE, CONV_DIM), lambda b, n: (b, 0, 0)),
                  pl.BlockSpec((1, N_HEADS_B, HEAD_DIM, HEAD_DIM), lambda b, n: (b, 0, 0, 0)),
                  pl.BlockSpec((CONV_W, CONV_DIM), lambda b, n: (0, 0)),
                  pl.BlockSpec((1, LANE), lambda b, n: (0, 0)),
                  pl.BlockSpec((1, LANE), lambda b, n: (0, 0)),
                  pl.BlockSpec((1, HEAD_DIM), lambda b, n: (0, 0))],
        out_specs=[pl.BlockSpec((n_valid, Z_B), lambda b, n: (b * nblk + n, 0)),
                   pl.BlockSpec((1, SUBLANE, CONV_DIM), lambda b, n: (b, 0, 0)),
                   pl.BlockSpec((1, N_HEADS_B, HEAD_DIM, HEAD_DIM), lambda b, n: (b, 0, 0, 0))],
        out_shape=[jax.ShapeDtypeStruct((B * T, Z_B), F32),
                   jax.ShapeDtypeStruct((B, SUBLANE, CONV_DIM), F32),
                   jax.ShapeDtypeStruct((B, N_HEADS_B, HEAD_DIM, HEAD_DIM), F32)],
        scratch_shapes=[pltpu.VMEM((SUBLANE + GDN_C, CONV_DIM), F32),
                        pltpu.VMEM((N_HEADS_B, HEAD_DIM, HEAD_DIM), F32)],
        compiler_params=_cparams(("arbitrary", "arbitrary")),
        name="gdn_mixer",
    )(proj, proj, proj, conv0, s0, conv_w, a_row, dt_row, norm_w)


R_EXP, R_GATE, R_RANK = 0, 4, 8


def _layer_norm(x, g, b):
    mu = jnp.mean(x, -1, keepdims=True)
    xc = x - mu
    var = jnp.mean(xc * xc, -1, keepdims=True)
    return xc * lax.rsqrt(var + LN_EPS) * g + b


def _oproj_body(oa_ref, ob_ref, x_ref, wa_ref, wb_ref, g_ref, b_ref, wr_ref, br_ref,
                h_ref, route_ref, cnt_ref, run_ref):
    i = pl.program_id(0)
    tm = x_ref.shape[0]

    @pl.when(i == 0)
    def _():
        run_ref[...] = jnp.zeros(run_ref.shape, F32)

    mix = _dot_bf(oa_ref[...], wa_ref[...]) + _dot_bf(ob_ref[...], wb_ref[...])
    h = _layer_norm(DEEPNORM_ALPHA * x_ref[...] + mix, g_ref[...], b_ref[...])
    h_ref[...] = h
    logits = _dot_f32(h, wr_ref[...]) + br_ref[...]
    lane = lax.broadcasted_iota(I32, (tm, LANE), 1)
    work = logits
    tops, idxs = [], []
    for _ in range(TOP_K):
        m = jnp.max(work, axis=1, keepdims=True)
        idx = jnp.min(jnp.where(work == m, lane, LANE), axis=1, keepdims=True)
        tops.append(m)
        idxs.append(idx)
        work = jnp.where(lane == idx, -jnp.inf, work)
    es = [jnp.exp(t - tops[0]) for t in tops]
    denom = es[0] + es[1] + es[2] + es[3]
    onehot = jnp.zeros((tm, LANE), F32)
    for idx in idxs:
        onehot = onehot + (lane == idx).astype(F32)
    r = lax.broadcasted_iota(I32, (tm, tm), 0)
    c = lax.broadcasted_iota(I32, (tm, tm), 1)
    before = jnp.dot((r > c).astype(BF16), onehot.astype(BF16), preferred_element_type=F32) + run_ref[...]
    rec = jnp.zeros((tm, LANE), F32)
    for k in range(TOP_K):
        rank = jnp.sum(jnp.where(lane == idxs[k], before, 0.0), axis=1, keepdims=True)
        rec = jnp.where(lane == R_EXP + k, idxs[k].astype(F32), rec)
        rec = jnp.where(lane == R_GATE + k, es[k] / denom, rec)
        rec = jnp.where(lane == R_RANK + k, rank, rec)
    route_ref[...] = rec
    run_ref[...] = run_ref[...] + jnp.sum(onehot, axis=0, keepdims=True)
    cnt_ref[...] = run_ref[...]


def _oproj_router(o_a, o_b, x, wa, wb, g1, b1, wr, br, tm):
    T = x.shape[0]
    row = lambda w: pl.BlockSpec((tm, w), lambda i: (i, 0))
    full = lambda a: pl.BlockSpec(a.shape, lambda i: (0,) * a.ndim)
    return pl.pallas_call(
        _oproj_body,
        grid=(T // tm,),
        in_specs=[row(o_a.shape[1]), row(o_b.shape[1]), row(D_MODEL),
                  full(wa), full(wb), full(g1), full(b1), full(wr), full(br)],
        out_specs=[row(D_MODEL), row(LANE), pl.BlockSpec((1, LANE), lambda i: (0, 0))],
        out_shape=[jax.ShapeDtypeStruct((T, D_MODEL), F32),
                   jax.ShapeDtypeStruct((T, LANE), F32),
                   jax.ShapeDtypeStruct((1, LANE), F32)],
        scratch_shapes=[pltpu.VMEM((1, LANE), F32)],
        compiler_params=_cparams(("arbitrary",)),
        name="out_proj_router",
    )(o_a, o_b, x, wa, wb, g1, b1, wr, br)


MOE_TM = 512
MOE_TF = 512
CMB_TN = 256


def _gather_body(tok_ref, h_hbm, o_hbm, sem, *, tm):
    b = pl.program_id(0)
    base = b * tm

    def issue(r, carry):
        t = tok_ref[base + r]
        pltpu.make_async_copy(h_hbm.at[pl.ds(t, 1)], o_hbm.at[pl.ds(base + r, 1)], sem).start()
        return carry

    lax.fori_loop(0, tm, issue, 0)

    def drain(r, carry):
        pltpu.make_async_copy(h_hbm.at[pl.ds(0, 1)], o_hbm.at[pl.ds(base + r, 1)], sem).wait()
        return carry

    lax.fori_loop(0, tm, drain, 0)


def _moe_gather(row_tok, h, n_rows, tm):
    grid_spec = pltpu.PrefetchScalarGridSpec(
        num_scalar_prefetch=1,
        grid=(n_rows // tm,),
        in_specs=[pl.BlockSpec(memory_space=pl.ANY)],
        out_specs=pl.BlockSpec(memory_space=pl.ANY),
        scratch_shapes=[pltpu.SemaphoreType.DMA(())],
    )
    return pl.pallas_call(
        functools.partial(_gather_body, tm=tm),
        grid_spec=grid_spec,
        out_shape=jax.ShapeDtypeStruct((n_rows, h.shape[1]), h.dtype),
        compiler_params=_cparams(("arbitrary",)),
        name="moe_gather",
    )(row_tok, h)


def _ffn_body(be_ref, nu_ref, x_ref, w1g_ref, w1l_ref, b1g_ref, b1l_ref, w2_ref, b2_ref, y_ref):
    b = pl.program_id(0)
    f = pl.program_id(1)

    @pl.when(b < nu_ref[0])
    def _():
        x = x_ref[...].astype(BF16)
        hg = jnp.dot(x, w1g_ref[0], preferred_element_type=F32) + b1g_ref[0]
        hl = jnp.dot(x, w1l_ref[0], preferred_element_type=F32) + b1l_ref[0]
        glu = jnp.minimum(hg, SWIGLU_LIMIT)
        lin = jnp.clip(hl, -SWIGLU_LIMIT, SWIGLU_LIMIT)
        act = glu * _sigmoid(SWIGLU_ALPHA * glu) * (lin + 1.0)
        part = jnp.dot(act.astype(BF16), w2_ref[0], preferred_element_type=F32)

        @pl.when(f == 0)
        def _():
            y_ref[...] = part + b2_ref[0]

        @pl.when(f > 0)
        def _():
            y_ref[...] = y_ref[...] + part

    @pl.when((b >= nu_ref[0]) & (f == 0))
    def _():
        y_ref[...] = jnp.zeros(y_ref.shape, F32)


def _moe_ffn(blk_e, n_used, x_rows, w1, b1, w2, b2, tm, tf):
    n_rows, D = x_rows.shape
    nf = D_FF // tf
    bb = lambda b, nu: jnp.minimum(b, nu[0] - 1)
    ff = lambda b, f, nu: jnp.where(b < nu[0], f, nf - 1)
    grid_spec = pltpu.PrefetchScalarGridSpec(
        num_scalar_prefetch=2,
        grid=(n_rows // tm, nf),
        in_specs=[pl.BlockSpec((tm, D), lambda b, f, be, nu: (bb(b, nu), 0)),
                  pl.BlockSpec((1, D, tf), lambda b, f, be, nu: (be[bb(b, nu)], 0, ff(b, f, nu))),
                  pl.BlockSpec((1, D, tf), lambda b, f, be, nu: (be[bb(b, nu)], 0, nf + ff(b, f, nu))),
                  pl.BlockSpec((1, 1, tf), lambda b, f, be, nu: (be[bb(b, nu)], 0, ff(b, f, nu))),
                  pl.BlockSpec((1, 1, tf), lambda b, f, be, nu: (be[bb(b, nu)], 0, nf + ff(b, f, nu))),
                  pl.BlockSpec((1, tf, D), lambda b, f, be, nu: (be[bb(b, nu)], ff(b, f, nu), 0)),
                  pl.BlockSpec((1, 1, D), lambda b, f, be, nu: (be[bb(b, nu)], 0, 0))],
        out_specs=pl.BlockSpec((tm, D), lambda b, f, be, nu: (b, 0)),
    )
    return pl.pallas_call(
        _ffn_body,
        grid_spec=grid_spec,
        out_shape=jax.ShapeDtypeStruct((n_rows, D), F32),
        compiler_params=_cparams(("arbitrary", "arbitrary")),
        name="moe_ffn",
    )(blk_e, n_used, x_rows, w1, w1, b1, b1, w2, b2)


def _combine_body(dest_ref, y_hbm, h_ref, route_ref, g_ref, b_ref, o_ref, buf_ref, sem):
    i = pl.program_id(0)
    tn = h_ref.shape[0]

    def issue(t, carry):
        for k in range(TOP_K):
            d = dest_ref[(i * tn + t) * TOP_K + k]
            pltpu.make_async_copy(y_hbm.at[pl.ds(d, 1)], buf_ref.at[k, pl.ds(t, 1)], sem).start()
        return carry

    lax.fori_loop(0, tn, issue, 0)

    def drain(t, carry):
        for k in range(TOP_K):
            pltpu.make_async_copy(y_hbm.at[pl.ds(0, 1)], buf_ref.at[k, pl.ds(t, 1)], sem).wait()
        return carry

    lax.fori_loop(0, tn, drain, 0)
    route = route_ref[...]
    f = route[:, R_GATE:R_GATE + 1] * buf_ref[0]
    for k in range(1, TOP_K):
        f = f + route[:, R_GATE + k:R_GATE + k + 1] * buf_ref[k]
    o_ref[...] = _layer_norm(DEEPNORM_ALPHA * h_ref[...] + f, g_ref[...], b_ref[...])


def _moe_combine(dest, y_rows, h, route, g2, b2, tn):
    T, D = h.shape
    grid_spec = pltpu.PrefetchScalarGridSpec(
        num_scalar_prefetch=1,
        grid=(T // tn,),
        in_specs=[pl.BlockSpec(memory_space=pl.ANY),
                  pl.BlockSpec((tn, D), lambda i, d: (i, 0)),
                  pl.BlockSpec((tn, LANE), lambda i, d: (i, 0)),
                  pl.BlockSpec((1, D), lambda i, d: (0, 0)),
                  pl.BlockSpec((1, D), lambda i, d: (0, 0))],
        out_specs=pl.BlockSpec((tn, D), lambda i, d: (i, 0)),
        scratch_shapes=[pltpu.VMEM((TOP_K, tn, D), F32), pltpu.SemaphoreType.DMA(())],
    )
    return pl.pallas_call(
        _combine_body,
        grid_spec=grid_spec,
        out_shape=jax.ShapeDtypeStruct((T, D), F32),
        compiler_params=_cparams(("arbitrary",)),
        name="moe_combine",
    )(dest, y_rows, h, route, g2, b2)


S_PG = 8
DEC_T = 8


def _page_copies(cache_hbm, buf_ref, sem, pt_ref, b, gi, slot, wait):
    for p in range(S_PG):
        src = cache_hbm.at[0] if wait else cache_hbm.at[pt_ref[b, gi * S_PG + p]]
        cp = pltpu.make_async_copy(src, buf_ref.at[slot, p], sem.at[slot])
        cp.wait() if wait else cp.start()


def _sidx_body(pt_ref, q_ref, w_ref, knew_ref, cache_hbm, o_ref, buf_ref, keys_ref, sem, *, topk, n_pages):
    b = pl.program_id(0)
    n_groups = n_pages // S_PG
    ncp = keys_ref.shape[0]
    q = q_ref[0]
    wl = jnp.broadcast_to(w_ref[0], (N_IDX_HEADS * DEC_T, LANE))

    def scores(kp):
        r = _dot_nt(q, kp)
        r = wl * jnp.maximum(r, 0.0)
        acc = r[0:DEC_T]
        for h in range(1, N_IDX_HEADS):
            acc = acc + r[h * DEC_T:(h + 1) * DEC_T]
        return acc

    _page_copies(cache_hbm, buf_ref, sem, pt_ref, b, 0, 0, False)

    def group(gi, carry):
        slot = gi % 2

        @pl.when(gi + 1 < n_groups)
        def _():
            _page_copies(cache_hbm, buf_ref, sem, pt_ref, b, gi + 1, 1 - slot, False)

        _page_copies(cache_hbm, buf_ref, sem, pt_ref, b, gi, slot, True)
        for p in range(S_PG):
            keys_ref[gi * S_PG + p] = _sortable(scores(buf_ref[slot, p]))
        return carry

    lax.fori_loop(0, n_groups, group, 0)
    row = lax.broadcasted_iota(I32, (DEC_T, LANE), 0)
    lane = lax.broadcasted_iota(I32, (DEC_T, LANE), 1)
    keys_ref[n_pages] = _sortable(jnp.where(lane <= row, scores(knew_ref[0]), -jnp.inf))
    for c in range(n_pages + 1, ncp):
        keys_ref[c] = jnp.full((DEC_T, LANE), INT_MIN, I32)

    U = 8

    def count(pred):
        def body(c, cnt):
            c0 = pl.multiple_of(c * U, U)
            kb = keys_ref[pl.ds(c0, U)]
            pos = (c0 + lax.broadcasted_iota(I32, (U, DEC_T, LANE), 0)) * LANE \
                + lax.broadcasted_iota(I32, (U, DEC_T, LANE), 2)
            return cnt + jnp.sum(pred(kb, pos).astype(I32), axis=0)

        cnt = lax.fori_loop(0, ncp // U, body, jnp.zeros((DEC_T, LANE), I32))
        return jnp.sum(cnt, axis=1, keepdims=True)

    thr = _kth_largest_key(lambda c: count(lambda kb, pos: kb >= c), (DEC_T, 1), topk)
    n_ge = count(lambda kb, pos: kb >= thr)
    n_gt = count(lambda kb, pos: kb > thr)
    need = jnp.where(thr == NEG_INF_KEY, 0, topk - n_gt)
    excess = jnp.max(n_ge - n_gt - need) > 0
    n_keys = (n_pages + 1) * LANE

    def tie_limit():
        nbits = max(1, int(math.ceil(math.log2(n_keys))))
        def bit_body(bi, p):
            cand = p + lax.shift_left(jnp.int32(1), nbits - 1 - bi)
            f = count(lambda kb, pos: (kb == thr) & (pos < cand))
            return jnp.where(f < need, cand, p)
        return lax.fori_loop(0, nbits, bit_body, jnp.zeros((DEC_T, 1), I32))

    lim = lax.cond(excess, tie_limit, lambda: jnp.full((DEC_T, 1), n_keys, I32))
    lim = jnp.where(need > 0, lim, -1)

    def write(c, carry):
        kb = keys_ref[c]
        pos = c * LANE + lane
        sel = (kb > thr) | ((kb == thr) & (pos <= lim))
        o_ref[0, c] = jnp.where(sel, 0.0, NEG)
        return carry

    lax.fori_loop(0, n_pages + 1, write, 0)


def _sample_indexer(page_table, q, w, k_new, cache_kidx, topk):
    DB, n_pages = page_table.shape
    ncp = (n_pages + 1 + 7) // 8 * 8
    grid_spec = pltpu.PrefetchScalarGridSpec(
        num_scalar_prefetch=1,
        grid=(DB,),
        in_specs=[pl.BlockSpec((1, N_IDX_HEADS * DEC_T, IDX_DIM), lambda b, pt: (b, 0, 0)),
                  pl.BlockSpec((1, N_IDX_HEADS * DEC_T, 1), lambda b, pt: (b, 0, 0)),
                  pl.BlockSpec((1, LANE, IDX_DIM), lambda b, pt: (b, 0, 0)),
                  pl.BlockSpec(memory_space=pl.ANY)],
        out_specs=pl.BlockSpec((1, n_pages + 1, DEC_T, LANE), lambda b, pt: (b, 0, 0, 0)),
        scratch_shapes=[pltpu.VMEM((2, S_PG, PAGE_SIZE, IDX_DIM), F32),
                        pltpu.VMEM((ncp, DEC_T, LANE), I32),
                        pltpu.SemaphoreType.DMA((2,))],
    )
    return pl.pallas_call(
        functools.partial(_sidx_body, topk=topk, n_pages=n_pages),
        grid_spec=grid_spec,
        out_shape=jax.ShapeDtypeStruct((DB, n_pages + 1, DEC_T, LANE), F32),
        compiler_params=_cparams(("arbitrary",)),
        name="sample_indexer",
    )(page_table, q, w, k_new, cache_kidx)


def _sattn_body(pt_ref, c31_ref, q_ref, knew_ref, vnew_ref, mask_ref, blast_ref, bnew_ref, ck_hbm, cv_hbm,
                o_ref, kbuf_ref, vbuf_ref, m_ref, l_ref, acc_ref, sem_k, sem_v, *, n_pages):
    b = pl.program_id(0)
    n_groups = n_pages // S_PG
    G = N_HEADS_A // N_KV_A
    m_ref[...] = jnp.full(m_ref.shape, NEG, F32)
    l_ref[...] = jnp.zeros(l_ref.shape, F32)
    acc_ref[...] = jnp.zeros(acc_ref.shape, F32)

    def block(kp, vp, mb, bias_of):
        mb2 = jnp.concatenate([mb] * G, axis=0)
        for g in range(N_KV_A):
            sl = slice(g * HEAD_DIM, (g + 1) * HEAD_DIM)
            lg = _dot_nt(q_ref[0, g], kp[:, sl]) + bias_of(g) + mb2
            m_old = m_ref[g]
            m_new = jnp.maximum(m_old, jnp.max(lg, axis=1, keepdims=True))
            alpha = jnp.exp(m_old - m_new)
            p = jnp.exp(lg - m_new)
            l_ref[g] = alpha * l_ref[g] + jnp.sum(p, axis=1, keepdims=True)
            m_ref[g] = m_new
            acc_ref[g] = alpha * acc_ref[g] + _dot_bf(p, vp[:, sl])

    def const_bias(g):
        rows = lax.broadcasted_iota(I32, (G * DEC_T, LANE), 0)
        out = jnp.full((G * DEC_T, LANE), c31_ref[g * G], F32)
        for r in range(1, G):
            out = jnp.where(rows >= r * DEC_T, c31_ref[g * G + r], out)
        return out

    _page_copies(ck_hbm, kbuf_ref, sem_k, pt_ref, b, 0, 0, False)
    _page_copies(cv_hbm, vbuf_ref, sem_v, pt_ref, b, 0, 0, False)

    def group(gi, carry):
        slot = gi % 2

        @pl.when(gi + 1 < n_groups)
        def _():
            _page_copies(ck_hbm, kbuf_ref, sem_k, pt_ref, b, gi + 1, 1 - slot, False)
            _page_copies(cv_hbm, vbuf_ref, sem_v, pt_ref, b, gi + 1, 1 - slot, False)

        _page_copies(ck_hbm, kbuf_ref, sem_k, pt_ref, b, gi, slot, True)
        _page_copies(cv_hbm, vbuf_ref, sem_v, pt_ref, b, gi, slot, True)
        for p in range(S_PG):
            pg = gi * S_PG + p
            is_last = pg == n_pages - 1
            block(kbuf_ref[slot, p], vbuf_ref[slot, p], mask_ref[0, pg],
                  lambda g: jnp.where(is_last, blast_ref[g], const_bias(g)))
        return carry

    lax.fori_loop(0, n_groups, group, 0)
    block(knew_ref[0], vnew_ref[0], mask_ref[0, n_pages], lambda g: bnew_ref[g])
    for g in range(N_KV_A):
        o_ref[0, g] = acc_ref[g] / l_ref[g]


def _sample_attention(page_table, c31, q, k_new, v_new, mask, b_last, b_new, cache_k, cache_v):
    DB, n_pages = page_table.shape
    G = N_HEADS_A // N_KV_A
    full = lambda a: pl.BlockSpec(a.shape, lambda b, pt, c: (0,) * a.ndim)
    grid_spec = pltpu.PrefetchScalarGridSpec(
        num_scalar_prefetch=2,
        grid=(DB,),
        in_specs=[pl.BlockSpec((1, N_KV_A, G * DEC_T, HEAD_DIM), lambda b, pt, c: (b, 0, 0, 0)),
                  pl.BlockSpec((1, LANE, KV_A), lambda b, pt, c: (b, 0, 0)),
                  pl.BlockSpec((1, LANE, KV_A), lambda b, pt, c: (b, 0, 0)),
                  pl.BlockSpec((1, n_pages + 1, DEC_T, LANE), lambda b, pt, c: (b, 0, 0, 0)),
                  full(b_last), full(b_new),
                  pl.BlockSpec(memory_space=pl.ANY), pl.BlockSpec(memory_space=pl.ANY)],
        out_specs=pl.BlockSpec((1, N_KV_A, G * DEC_T, HEAD_DIM), lambda b, pt, c: (b, 0, 0, 0)),
        scratch_shapes=[pltpu.VMEM((2, S_PG, PAGE_SIZE, KV_A), F32),
                        pltpu.VMEM((2, S_PG, PAGE_SIZE, KV_A), F32),
                        pltpu.VMEM((N_KV_A, G * DEC_T, 1), F32),
                        pltpu.VMEM((N_KV_A, G * DEC_T, 1), F32),
                        pltpu.VMEM((N_KV_A, G * DEC_T, HEAD_DIM), F32),
                        pltpu.SemaphoreType.DMA((2,)),
                        pltpu.SemaphoreType.DMA((2,))],
    )
    return pl.pallas_call(
        functools.partial(_sattn_body, n_pages=n_pages),
        grid_spec=grid_spec,
        out_shape=jax.ShapeDtypeStruct((DB, N_KV_A, G * DEC_T, HEAD_DIM), F32),
        compiler_params=_cparams(("arbitrary",)),
        name="sample_attention",
    )(page_table, c31, q, k_new, v_new, mask, b_last, b_new, cache_k, cache_v)


def _finish_layer(x, o_a, o_b, w_out, ln1_g, ln1_b, w_router, b_router, w1, b1, w2, b2, ln2_g, ln2_b):
    T = x.shape[0]
    E = w1.shape[0]
    half = o_a.shape[1]
    wa = w_out[:half].astype(BF16)
    wb = w_out[half:].astype(BF16)
    wr = jnp.zeros((D_MODEL, LANE), F32).at[:, :E].set(w_router.astype(F32))
    br = jnp.full((1, LANE), NEG, F32).at[0, :E].set(b_router.astype(F32))
    row2 = lambda a: a.astype(F32).reshape(1, -1)
    h, route, counts = _oproj_router(o_a, o_b, x, wa, wb, row2(ln1_g), row2(ln1_b), wr, br,
                                     tm=_pick(T, 256, SUBLANE))
    tm = MOE_TM
    top_e = route[:, R_EXP:R_EXP + TOP_K].astype(I32)
    rank = route[:, R_RANK:R_RANK + TOP_K].astype(I32)
    cnt = counts[0, :E].astype(I32)
    padded = (cnt + tm - 1) // tm * tm
    pend = jnp.cumsum(padded)
    pstart = pend - padded
    dest = (pstart[top_e] + rank).reshape(-1)
    n_blocks = (T * TOP_K + E * (tm - 1) + tm - 1) // tm
    tok = jnp.arange(T * TOP_K, dtype=I32) // TOP_K
    row_tok = jnp.zeros((n_blocks * tm,), I32).at[dest].set(tok)
    blk_e = jnp.minimum(jnp.searchsorted(pend, jnp.arange(n_blocks, dtype=I32) * tm, side='right'),
                        E - 1).astype(I32)
    n_used = (pend[-1:] // tm).astype(I32)
    x_rows = _moe_gather(row_tok, h, n_blocks * tm, tm)
    y_rows = _moe_ffn(blk_e, n_used, x_rows, w1.astype(BF16), b1.astype(F32).reshape(E, 1, -1),
                      w2.astype(BF16), b2.astype(F32).reshape(E, 1, -1), tm, MOE_TF)
    return _moe_combine(dest, y_rows, h, route, row2(ln2_g), row2(ln2_b), _pick(T, CMB_TN, SUBLANE))


def _pick(n, cap, mult):
    best = None
    for d in range(mult, min(n, cap) + 1, mult):
        if n % d == 0:
            best = d
    assert best is not None, (n, cap, mult)
    return best


def kernel(x_prompt, x_sample, cache_k, cache_v, cache_kidx, state_conv, state_ssm, page_table, w_in, rel_table,
           w_out, conv_w, A_log, dt_bias, gdn_norm_w, ln1_g, ln1_b, w_router, b_router, w1, b1, w2, b2,
           ln2_g, ln2_b):
    BP, T, D = x_prompt.shape
    DB, TS, _ = x_sample.shape
    n_pool = cache_k.shape[0]
    n_pages = page_table.shape[1]
    past = n_pages * PAGE_SIZE
    assert BP == 1 and D == D_MODEL and TS == DEC_T and T % ATT_TK == 0 and n_pages % S_PG == 0
    G = N_HEADS_A // N_KV_A
    nq = T // LANE
    att_scale = HEAD_DIM ** -0.5
    idx_scale = IDX_DIM ** -0.5 * N_IDX_HEADS ** -0.5

    x_all = jnp.concatenate([x_prompt.reshape(T, D), x_sample.reshape(DB * TS, D)], axis=0).astype(F32)
    TA = x_all.shape[0]
    proj = _proj(x_all, _arrange_w_in(w_in), _pick(TA, 1280, SUBLANE), 768)
    bias = _bias_tiles(rel_table)
    c31 = rel_table[N_BUCKETS - 1].astype(F32)

    pp = proj[:T]
    q_t = (pp[:, C_QA:C_QA + Q_A].reshape(nq, LANE, N_KV_A, G, HEAD_DIM) * att_scale)
    q_t = q_t.transpose(0, 2, 4, 3, 1).reshape(nq, N_KV_A, HEAD_DIM, G * LANE).astype(BF16)
    k_p = pp[:, C_KA:C_KA + KV_A]
    v_p = pp[:, C_VA:C_VA + KV_A]
    ki_p = pp[:, C_MISC + M_KI:C_MISC + M_KI + IDX_DIM]
    qi_t = pp[:, C_QI:C_QI + QI_W].reshape(nq, LANE, N_IDX_HEADS // 2, 2, IDX_DIM)
    qi_t = qi_t.transpose(0, 2, 4, 3, 1).reshape(nq, N_IDX_HEADS // 2, IDX_DIM, 2 * LANE).astype(BF16)
    w_t = (pp[:, C_MISC + M_WI:C_MISC + M_WI + N_IDX_HEADS] * idx_scale).reshape(nq, LANE, N_IDX_HEADS)
    w_t = w_t.transpose(0, 2, 1)
    mask_p = _prompt_indexer(qi_t, w_t, ki_p.astype(BF16), min(TOPK_MAX, T // 4))
    o_t = _prompt_attention(q_t, k_p.astype(BF16), v_p.T.astype(BF16), mask_p, bias, c31)
    oa_p = o_t.reshape(nq, N_KV_A, HEAD_DIM, G, LANE).transpose(0, 4, 1, 3, 2).reshape(T, Q_A)

    ps = proj[T:].reshape(DB, TS, N_PROJ)
    pad_rows = lambda a: jnp.pad(a, ((0, 0), (0, LANE - TS), (0, 0)))
    q_s = (ps[..., C_QA:C_QA + Q_A].reshape(DB, TS, N_KV_A, G, HEAD_DIM) * att_scale)
    q_s = q_s.transpose(0, 2, 3, 1, 4).reshape(DB, N_KV_A, G * TS, HEAD_DIM).astype(BF16)
    k_s = ps[..., C_KA:C_KA + KV_A]
    v_s = ps[..., C_VA:C_VA + KV_A]
    ki_s = ps[..., C_MISC + M_KI:C_MISC + M_KI + IDX_DIM]
    qi_s = ps[..., C_QI:C_QI + QI_W].reshape(DB, TS, N_IDX_HEADS, IDX_DIM)
    qi_s = qi_s.transpose(0, 2, 1, 3).reshape(DB, N_IDX_HEADS * TS, IDX_DIM).astype(BF16)
    w_s = (ps[..., C_MISC + M_WI:C_MISC + M_WI + N_IDX_HEADS] * idx_scale).transpose(0, 2, 1)
    w_s = w_s.reshape(DB, N_IDX_HEADS * TS, 1)
    mask_s = _sample_indexer(page_table, qi_s, w_s, pad_rows(ki_s), cache_kidx.astype(F32),
                             min(TOPK_MAX, (past + TS) // 4))
    to_rows = lambda t: t[:, :, :TS].transpose(0, 2, 1).reshape(N_KV_A, G * TS, LANE)
    o_s = _sample_attention(page_table, c31, q_s, pad_rows(k_s), pad_rows(v_s), mask_s,
                            to_rows(bias[1]), to_rows(bias[0]),
                            cache_k.astype(F32).reshape(n_pool, PAGE_SIZE, KV_A),
                            cache_v.astype(F32).reshape(n_pool, PAGE_SIZE, KV_A))
    oa_s = o_s.reshape(DB, N_KV_A, G, TS, HEAD_DIM).transpose(0, 3, 1, 2, 4).reshape(DB * TS, Q_A)

    lanes8 = lambda a: jnp.zeros((1, LANE), F32).at[0, M_A:M_A + N_HEADS_B].set(a.astype(F32))
    gdn_args = (conv_w.astype(F32), lanes8(A_log), lanes8(dt_bias), gdn_norm_w.astype(F32).reshape(1, HEAD_DIM))
    ob_p, cb_p, s_p = _gdn(proj, 0, BP, T, jnp.zeros((BP, SUBLANE, CONV_DIM), F32),
                           jnp.zeros((BP, N_HEADS_B, HEAD_DIM, HEAD_DIM), F32), *gdn_args)
    conv0_s = jnp.pad(state_conv.astype(F32), ((0, 0), (SUBLANE - (CONV_W - 1), 0), (0, 0)))
    ob_s, cb_s, s_s = _gdn(proj, T, DB, TS, conv0_s, state_ssm.astype(F32), *gdn_args)

    y_all = _finish_layer(x_all, jnp.concatenate([oa_p, oa_s], axis=0), jnp.concatenate([ob_p, ob_s], axis=0),
                          w_out, ln1_g, ln1_b, w_router, b_router, w1, b1, w2, b2, ln2_g, ln2_b)
    tail = SUBLANE - (CONV_W - 1)
    return (y_all[:T].reshape(BP, T, D), y_all[T:].reshape(DB, TS, D),
            k_p.reshape(BP, T, N_KV_A, HEAD_DIM), v_p.reshape(BP, T, N_KV_A, HEAD_DIM),
            ki_p.reshape(BP, T, IDX_DIM), cb_p[:, tail:], s_p,
            k_s.reshape(DB, TS, N_KV_A, HEAD_DIM), v_s.reshape(DB, TS, N_KV_A, HEAD_DIM), ki_s,
            cb_s[:, tail:], s_s)
```

```python
import functools
import math

import jax
import jax.numpy as jnp
import numpy as np
from jax import lax
from jax.experimental import pallas as pl
from jax.experimental.pallas import tpu as pltpu

F32 = jnp.float32
BF16 = jnp.bfloat16
I32 = jnp.int32

D_MODEL = 2048
HEAD_DIM = 128
N_HEADS_A = 8
N_KV_A = 4
N_IDX_HEADS = 16
IDX_DIM = 64
TOPK_MAX = 256
N_BUCKETS = 32
MAX_DISTANCE = 128
N_HEADS_B = 8
CONV_W = 4
CONV_DIM = N_HEADS_B * 3 * HEAD_DIM
TOP_K = 4
D_FF = D_MODEL
SWIGLU_LIMIT = 7.0
SWIGLU_ALPHA = 1.702
PAGE_SIZE = 128
DEPTH = 1
DEEPNORM_ALPHA = (2 * DEPTH) ** 0.25
LN_EPS = 1e-5

Q_A = N_HEADS_A * HEAD_DIM
KV_A = N_KV_A * HEAD_DIM
QI_W = N_IDX_HEADS * IDX_DIM
Z_B = N_HEADS_B * HEAD_DIM

LANE = 128
SUBLANE = 8
VMEM_LIMIT = 56 * 1024 * 1024

C_QA = 0
C_KA = C_QA + Q_A
C_VA = C_KA + KV_A
C_QI = C_VA + KV_A
C_QKV = C_QI + QI_W
C_Z = C_QKV + CONV_DIM
C_MISC = C_Z + Z_B
M_KI, M_WI, M_A, M_B = 0, 64, 80, 88
N_PROJ = 7680

NEG = -1e30
GDN_C = 128


def _cparams(sem, vmem=VMEM_LIMIT):
    return pltpu.CompilerParams(dimension_semantics=sem, vmem_limit_bytes=vmem)


def _proj_body(x_ref, w_ref, o_ref, xb_ref):
    @pl.when(pl.program_id(1) == 0)
    def _():
        xb_ref[...] = x_ref[...].astype(BF16)

    o_ref[...] = jnp.dot(xb_ref[...], w_ref[...], preferred_element_type=F32)


def _proj(x, w, tm, tn):
    T, D = x.shape
    N = w.shape[1]
    return pl.pallas_call(
        _proj_body,
        grid=(T // tm, N // tn),
        in_specs=[pl.BlockSpec((tm, D), lambda i, j: (i, 0)),
                  pl.BlockSpec((D, tn), lambda i, j: (0, j))],
        out_specs=pl.BlockSpec((tm, tn), lambda i, j: (i, j)),
        out_shape=jax.ShapeDtypeStruct((T, N), F32),
        scratch_shapes=[pltpu.VMEM((tm, D), BF16)],
        compiler_params=_cparams(("arbitrary", "arbitrary")),
        name="in_proj",
    )(x, w)


def _arrange_w_in(w_in):
    offs = np.cumsum([0, Q_A, KV_A, KV_A, QI_W, IDX_DIM, N_IDX_HEADS, CONV_DIM, Z_B, N_HEADS_B, N_HEADS_B])
    seg = lambda k: w_in[:, offs[k]:offs[k + 1]]
    qa, ka, va, qi, ki, wi, qkv, z, a, b = (seg(k) for k in range(10))
    pad_misc = jnp.zeros((w_in.shape[0], LANE - (IDX_DIM + N_IDX_HEADS + 2 * N_HEADS_B)), w_in.dtype)
    pad_tail = jnp.zeros((w_in.shape[0], N_PROJ - (C_MISC + LANE)), w_in.dtype)
    return jnp.concatenate([qa, ka, va, qi, qkv, z, ki, wi, a, b, pad_misc, pad_tail], axis=1).astype(BF16)


def _bias_body(rel_ref, o_ref):
    d = pl.program_id(0)
    s = lax.broadcasted_iota(I32, (LANE, LANE), 0)
    t = lax.broadcasted_iota(I32, (LANE, LANE), 1)
    n = jnp.maximum(d * LANE + t - s, 0)
    max_exact = N_BUCKETS // 2
    nf = jnp.maximum(n, 1).astype(F32)
    large = max_exact + (jnp.log(nf / max_exact) / math.log(MAX_DISTANCE / max_exact)
                         * (N_BUCKETS - max_exact)).astype(I32)
    large = jnp.minimum(large, N_BUCKETS - 1)
    bucket = jnp.where(n < max_exact, n, large)
    for h in range(N_HEADS_A):
        acc = jnp.zeros((LANE, LANE), F32)
        for b in range(N_BUCKETS):
            acc = jnp.where(bucket == b, rel_ref[b, h], acc)
        o_ref[0, h] = acc


def _bias_tiles(rel_table):
    return pl.pallas_call(
        _bias_body,
        grid=(3,),
        in_specs=[pl.BlockSpec(memory_space=pltpu.SMEM)],
        out_specs=pl.BlockSpec((1, N_HEADS_A, LANE, LANE), lambda d: (d, 0, 0, 0)),
        out_shape=jax.ShapeDtypeStruct((3, N_HEADS_A, LANE, LANE), F32),
        compiler_params=_cparams(("arbitrary",)),
        name="rel_bias_tiles",
    )(rel_table.astype(F32))


INT_MIN = -2 ** 31
NEG_INF_KEY = int(np.int32(np.array(-np.inf, np.float32).view(np.int32)) ^ np.int32(0x7FFFFFFF))


def _sortable(x):
    b = pltpu.bitcast(x, I32)
    return b ^ (lax.shift_right_arithmetic(b, 31) & jnp.int32(0x7FFFFFFF))


def _kth_largest_key(count_ge, shape, k):
    def bit_body(b, c):
        cand = c + lax.shift_left(jnp.int32(1), 31 - b)
        return jnp.where(count_ge(cand) >= k, cand, c)

    return lax.fori_loop(0, 32, bit_body, jnp.full(shape, INT_MIN, I32))


IDX_CH = 512


def _pidx_body(qi_ref, misc_ref, ki_ref, o_ref, keys_ref, qt_ref, w_ref, *, topk, n_keys):
    i = pl.program_id(0)
    nch = (i * LANE + LANE + IDX_CH - 1) // IDX_CH
    t_idx = i * LANE + lax.broadcasted_iota(I32, (IDX_CH, LANE), 1)
    s_loc = lax.broadcasted_iota(I32, (IDX_CH, LANE), 0)

    for p in range(N_IDX_HEADS // 2):
        xt = qi_ref[:, p * LANE:(p + 1) * LANE].T
        qt_ref[p] = jnp.concatenate([xt[:IDX_DIM], xt[IDX_DIM:]], axis=1).astype(BF16)
    w_ref[...] = misc_ref[...].T * (IDX_DIM ** -0.5 * N_IDX_HEADS ** -0.5)

    def score_chunk(c, carry):
        k0 = pl.multiple_of(c * IDX_CH, IDX_CH)
        kc = ki_ref[pl.ds(k0, IDX_CH), :]
        acc = jnp.zeros((IDX_CH, LANE), F32)
        for p in range(N_IDX_HEADS // 2):
            r = jnp.dot(kc, qt_ref[p], preferred_element_type=F32)
            acc = acc + w_ref[M_WI + 2 * p:M_WI + 2 * p + 1, :] * jnp.maximum(r[:, :LANE], 0.0)
            acc = acc + w_ref[M_WI + 2 * p + 1:M_WI + 2 * p + 2, :] * jnp.maximum(r[:, LANE:], 0.0)
        acc = jnp.where(k0 + s_loc <= t_idx, acc, -jnp.inf)
        keys_ref[pl.ds(k0, IDX_CH), :] = _sortable(acc)
        return carry

    lax.fori_loop(0, nch, score_chunk, 0)

    def count(pred):
        def body(c, cnt):
            k0 = pl.multiple_of(c * IDX_CH, IDX_CH)
            m = pred(keys_ref[pl.ds(k0, IDX_CH), :], k0).astype(I32)
            return cnt + jnp.sum(m.reshape(IDX_CH // SUBLANE, SUBLANE, LANE), axis=0)

        cnt = lax.fori_loop(0, nch, body, jnp.zeros((SUBLANE, LANE), I32))
        return jnp.sum(cnt, axis=0, keepdims=True)

    thr = _kth_largest_key(lambda c: count(lambda kb, k0: kb >= c), (1, LANE), topk)
    n_ge = count(lambda kb, k0: kb >= thr)
    n_gt = count(lambda kb, k0: kb > thr)
    need = jnp.where(thr == NEG_INF_KEY, 0, topk - n_gt)
    excess = jnp.max(n_ge - n_gt - need) > 0

    def tie_limit():
        nbits = max(1, int(math.ceil(math.log2(n_keys))))
        def bit_body(b, p):
            cand = p + lax.shift_left(jnp.int32(1), nbits - 1 - b)
            f = count(lambda kb, k0: (kb == thr) & (k0 + s_loc < cand))
            return jnp.where(f < need, cand, p)
        return lax.fori_loop(0, nbits, bit_body, jnp.zeros((1, LANE), I32))

    lim = lax.cond(excess, tie_limit, lambda: jnp.full((1, LANE), n_keys, I32))
    lim = jnp.where(need > 0, lim, -1)

    def write_chunk(c, carry):
        k0 = pl.multiple_of(c * IDX_CH, IDX_CH)
        kb = keys_ref[pl.ds(k0, IDX_CH), :]
        sel = (kb > thr) | ((kb == thr) & (k0 + s_loc <= lim))
        o_ref[pl.ds(k0, IDX_CH), :] = jnp.where(sel, 0.0, NEG).astype(o_ref.dtype)
        return carry

    lax.fori_loop(0, nch, write_chunk, 0)

    def fill_chunk(c, carry):
        k0 = pl.multiple_of(c * IDX_CH, IDX_CH)
        o_ref[pl.ds(k0, IDX_CH), :] = jnp.full((IDX_CH, LANE), NEG, o_ref.dtype)
        return carry

    lax.fori_loop(nch, n_keys // IDX_CH, fill_chunk, 0)


def _prompt_indexer(proj, ki, topk):
    T = ki.shape[0]
    return pl.pallas_call(
        functools.partial(_pidx_body, topk=topk, n_keys=T),
        grid=(T // LANE,),
        in_specs=[pl.BlockSpec((LANE, QI_W), lambda i: (i, C_QI // QI_W)),
                  pl.BlockSpec((LANE, LANE), lambda i: (i, C_MISC // LANE)),
                  pl.BlockSpec((T, IDX_DIM), lambda i: (0, 0))],
        out_specs=pl.BlockSpec((T, LANE), lambda i: (0, i)),
        out_shape=jax.ShapeDtypeStruct((T, T), BF16),
        scratch_shapes=[pltpu.VMEM((T, LANE), I32),
                        pltpu.VMEM((N_IDX_HEADS // 2, IDX_DIM, 2 * LANE), BF16),
                        pltpu.VMEM((LANE, LANE), F32)],
        compiler_params=_cparams(("arbitrary",)),
        name="prompt_indexer",
    )(proj, proj, ki)


ATT_TK = 512


def _pattn_body(ii_ref, jj_ref, q_ref, k_ref, vt_ref, mask_ref, bias_ref, o_ref, qt_ref, m_ref, l_ref, acc_ref):
    s = pl.program_id(0)
    i = ii_ref[s]
    j = jj_ref[s]
    G = N_HEADS_A // N_KV_A
    NU = ATT_TK // LANE

    @pl.when(j == 0)
    def _():
        m_ref[...] = jnp.full(m_ref.shape, NEG, F32)
        l_ref[...] = jnp.zeros(l_ref.shape, F32)
        acc_ref[...] = jnp.zeros(acc_ref.shape, F32)
        for h in range(N_HEADS_A):
            qh = q_ref[:, h * HEAD_DIM:(h + 1) * HEAD_DIM] * HEAD_DIM ** -0.5
            qt_ref[h // G, :, (h % G) * LANE:(h % G + 1) * LANE] = qh.T.astype(BF16)

    mb = mask_ref[...].astype(F32)
    bidx = [jnp.clip((i * LANE - j * ATT_TK - u * LANE) // LANE, 0, 2) for u in range(NU)]
    for g in range(N_KV_A):
        st = jnp.dot(k_ref[:, g * HEAD_DIM:(g + 1) * HEAD_DIM], qt_ref[g], preferred_element_type=F32)
        cols = []
        for r in range(G):
            add = jnp.concatenate(
                [bias_ref[bidx[u], g * G + r] + mb[u * LANE:(u + 1) * LANE] for u in range(NU)], axis=0)
            cols.append(st[:, r * LANE:(r + 1) * LANE] + add)
        lg = jnp.concatenate(cols, axis=1)
        m_old = m_ref[g]
        m_new = jnp.maximum(m_old, jnp.max(lg, axis=0, keepdims=True))
        alpha = jnp.exp(m_old - m_new)
        p = jnp.exp(lg - m_new)
        l_ref[g] = alpha * l_ref[g] + jnp.sum(p, axis=0, keepdims=True)
        m_ref[g] = m_new
        vg = vt_ref[g * HEAD_DIM:(g + 1) * HEAD_DIM, :]
        acc_ref[g] = alpha * acc_ref[g] + jnp.dot(vg, p.astype(BF16), preferred_element_type=F32)

    @pl.when((j + 1) * ATT_TK > i * LANE + LANE - 1)
    def _():
        for h in range(N_HEADS_A):
            cs = slice((h % G) * LANE, (h % G + 1) * LANE)
            o_ref[:, h * HEAD_DIM:(h + 1) * HEAD_DIM] = (acc_ref[h // G][:, cs] / l_ref[h // G][:, cs]).T


def _prompt_attention(proj, k, v_t, mask, bias):
    T = k.shape[0]
    nq = T // LANE
    ii, jj = [], []
    for i in range(nq):
        for j in range((i * LANE + LANE - 1) // ATT_TK + 1):
            ii.append(i)
            jj.append(j)
    ii = jnp.asarray(np.array(ii, np.int32))
    jj = jnp.asarray(np.array(jj, np.int32))
    G = N_HEADS_A // N_KV_A
    grid_spec = pltpu.PrefetchScalarGridSpec(
        num_scalar_prefetch=2,
        grid=(int(ii.shape[0]),),
        in_specs=[pl.BlockSpec((LANE, Q_A), lambda s, ii, jj: (ii[s], C_QA // Q_A)),
                  pl.BlockSpec((ATT_TK, KV_A), lambda s, ii, jj: (jj[s], 0)),
                  pl.BlockSpec((KV_A, ATT_TK), lambda s, ii, jj: (0, jj[s])),
                  pl.BlockSpec((ATT_TK, LANE), lambda s, ii, jj: (jj[s], ii[s])),
                  pl.BlockSpec((3, N_HEADS_A, LANE, LANE), lambda s, ii, jj: (0, 0, 0, 0))],
        out_specs=pl.BlockSpec((LANE, Q_A), lambda s, ii, jj: (ii[s], 0)),
        scratch_shapes=[pltpu.VMEM((N_KV_A, HEAD_DIM, G * LANE), BF16),
                        pltpu.VMEM((N_KV_A, 1, G * LANE), F32),
                        pltpu.VMEM((N_KV_A, 1, G * LANE), F32),
                        pltpu.VMEM((N_KV_A, HEAD_DIM, G * LANE), F32)],
    )
    return pl.pallas_call(
        _pattn_body,
        grid_spec=grid_spec,
        out_shape=jax.ShapeDtypeStruct((T, Q_A), F32),
        compiler_params=_cparams(("arbitrary",)),
        name="prompt_attention",
    )(ii, jj, proj, k, v_t, mask, bias)


def _split3(a):
    a1 = a.astype(BF16)
    r = a - a1.astype(F32)
    a2 = r.astype(BF16)
    a3 = (r - a2.astype(F32)).astype(BF16)
    return a1, a2, a3


def _dot_f32(a, b):
    a1, a2, a3 = _split3(a)
    b1, b2, b3 = _split3(b)
    d = lambda p, q: jnp.dot(p, q, preferred_element_type=F32)
    return (d(a1, b1) + (d(a1, b2) + d(a2, b1))) + ((d(a1, b3) + d(a3, b1)) + d(a2, b2))


def _split2(a):
    a1 = a.astype(BF16)
    return a1, (a - a1.astype(F32)).astype(BF16)


def _dot_3p(a, b):
    a1, a2 = _split2(a)
    b1, b2 = _split2(b)
    d = lambda p, q: jnp.dot(p, q, preferred_element_type=F32)
    return d(a1, b1) + (d(a1, b2) + d(a2, b1))


def _dot_bf(a, b):
    return jnp.dot(a.astype(BF16), b.astype(BF16), preferred_element_type=F32)


def _dot_nt(a, b):
    return lax.dot_general(a.astype(BF16), b.astype(BF16), (((1,), (1,)), ((), ())), preferred_element_type=F32)


def _sigmoid(x):
    return 1.0 / (1.0 + jnp.exp(-x))


def _softplus(x):
    return jnp.maximum(x, 0.0) + jnp.log(1.0 + jnp.exp(-jnp.abs(x)))


def _gdn_body(qkv_ref, z_ref, misc_ref, conv0_ref, s0_ref, cw_ref, arow_ref, dtrow_ref, nw_ref,
              o_ref, convo_ref, so_ref, xbuf_ref, s_ref, *, n_valid):
    C = GDN_C
    n = pl.program_id(1)
    H = N_HEADS_B

    @pl.when(n == 0)
    def _():
        xbuf_ref[0:SUBLANE, :] = conv0_ref[0]
        s_ref[...] = s0_ref[0]
        if n_valid < C:
            xbuf_ref[SUBLANE:, :] = jnp.zeros((C, CONV_DIM), F32)

    xbuf_ref[SUBLANE:SUBLANE + n_valid, :] = qkv_ref[...]
    conv = xbuf_ref[5:5 + C, :] * cw_ref[0:1, :]
    for i in range(1, CONV_W):
        conv = conv + xbuf_ref[5 + i:5 + i + C, :] * cw_ref[i:i + 1, :]
    tail = xbuf_ref[n_valid:n_valid + SUBLANE, :]
    xbuf_ref[0:SUBLANE, :] = tail
    convo_ref[0] = tail

    row = lax.broadcasted_iota(I32, (C, LANE), 0)
    col = lax.broadcasted_iota(I32, (C, LANE), 1)
    live = row < n_valid
    hact = conv * _sigmoid(conv)

    misc = misc_ref[...]
    if n_valid < C:
        misc = jnp.concatenate([misc, jnp.zeros((C - n_valid, LANE), F32)], axis=0)
    g_full = jnp.where(live, -jnp.exp(arow_ref[...]) * _softplus(misc + dtrow_ref[...]), 0.0)
    beta_full = jnp.where(live, _sigmoid(misc), 0.0)
    tril = (row >= col).astype(F32)
    gc_full = _dot_f32(tril, g_full)
    gc_t = gc_full.T
    strict = row > col
    incl = row >= col

    for hd in range(H):
        sl = slice(hd * HEAD_DIM, (hd + 1) * HEAD_DIM)
        q = hact[:, sl]
        k = hact[:, H * HEAD_DIM + hd * HEAD_DIM:H * HEAD_DIM + (hd + 1) * HEAD_DIM]
        v = hact[:, 2 * H * HEAD_DIM + hd * HEAD_DIM:2 * H * HEAD_DIM + (hd + 1) * HEAD_DIM]
        q = q * lax.rsqrt(jnp.sum(q * q, -1, keepdims=True) + 1e-6) * HEAD_DIM ** -0.5
        k = k * lax.rsqrt(jnp.sum(k * k, -1, keepdims=True) + 1e-6)
        q = jnp.where(live, q, 0.0)
        k = jnp.where(live, k, 0.0)
        gc_col = jnp.broadcast_to(gc_full[:, M_A + hd:M_A + hd + 1], (C, LANE))
        gc_row = jnp.broadcast_to(gc_t[M_A + hd:M_A + hd + 1, :], (C, LANE))
        gc_last = jnp.broadcast_to(gc_full[C - 1:C, M_A + hd:M_A + hd + 1], (C, LANE))
        beta = jnp.broadcast_to(beta_full[:, M_B + hd:M_B + hd + 1], (C, LANE))
        decay = jnp.where(incl, jnp.exp(jnp.where(incl, gc_col - gc_row, 0.0)), 0.0)
        kk = _dot_nt(k, k)
        qk = _dot_nt(q, k) * decay
        e_gc = jnp.exp(gc_col)
        m = jnp.where(strict, -(beta * decay * kk), 0.0)
        y = jnp.concatenate([v * beta, k * (beta * e_gc)], axis=1)
        nlev = int(math.log2(C))
        for lvl in range(nlev):
            if lvl + 1 < nlev:
                mp = _dot_3p(m, jnp.concatenate([m, y], axis=1))
                y = y + mp[:, C:]
                m = mp[:, :C]
            else:
                y = y + _dot_3p(m, y)
        u0 = y[:, :HEAD_DIM]
        w = y[:, HEAD_DIM:]
        s_old = s_ref[hd]
        u = u0 - _dot_bf(w, s_old)
        o = _dot_bf(q * e_gc, s_old) + _dot_bf(qk, u)
        kd = k * jnp.exp(gc_last - gc_col)
        s_ref[hd] = jnp.exp(gc_last) * s_old + _dot_bf(kd.T, u)
        o = o * lax.rsqrt(jnp.mean(o * o, -1, keepdims=True) + 1e-6) * nw_ref[...]
        zz = z_ref[:, sl]
        o_ref[:, sl] = o[:n_valid] * (zz * _sigmoid(zz))

    so_ref[0] = s_ref[...]


def _gdn(proj, row0, B, T, conv0, s0, conv_w, a_row, dt_row, norm_w):
    n_valid = min(T, GDN_C)
    nblk = T // n_valid
    r0 = row0 // n_valid
    rb = lambda b, n: r0 + b * nblk + n
    return pl.pallas_call(
        functools.partial(_gdn_body, n_valid=n_valid),
        grid=(B, nblk),
        in_specs=[pl.BlockSpec((n_valid, CONV_DIM), lambda b, n: (rb(b, n), C_QKV // CONV_DIM)),
                  pl.BlockSpec((n_valid, Z_B), lambda b, n: (rb(b, n), C_Z // Z_B)),
                  pl.BlockSpec((n_valid, LANE), lambda b, n: (rb(b, n), C_MISC // LANE)),
                  pl.BlockSpec((1, SUBLANE, CONV_DIM), lambda b, n: (b, 0, 0)),
                  pl.BlockSpec((1, N_HEADS_B, HEAD_DIM, HEAD_DIM), lambda b, n: (b, 0, 0, 0)),
                  pl.BlockSpec((CONV_W, CONV_DIM), lambda b, n: (0, 0)),
                  pl.BlockSpec((1, LANE), lambda b, n: (0, 0)),
                  pl.BlockSpec((1, LANE), lambda b, n: (0, 0)),
                  pl.BlockSpec((1, HEAD_DIM), lambda b, n: (0, 0))],
        out_specs=[pl.BlockSpec((n_valid, Z_B), lambda b, n: (b * nblk + n, 0)),
                   pl.BlockSpec((1, SUBLANE, CONV_DIM), lambda b, n: (b, 0, 0)),
                   pl.BlockSpec((1, N_HEADS_B, HEAD_DIM, HEAD_DIM), lambda b, n: (b, 0, 0, 0))],
        out_shape=[jax.ShapeDtypeStruct((B * T, Z_B), F32),
                   jax.ShapeDtypeStruct((B, SUBLANE, CONV_DIM), F32),
                   jax.ShapeDtypeStruct((B, N_HEADS_B, HEAD_DIM, HEAD_DIM), F32)],
        scratch_shapes=[pltpu.VMEM((SUBLANE + GDN_C, CONV_DIM), F32),
                        pltpu.VMEM((N_HEADS_B, HEAD_DIM, HEAD_DIM), F32)],
        compiler_params=_cparams(("arbitrary", "arbitrary")),
        name="gdn_mixer",
    )(proj, proj, proj, conv0, s0, conv_w, a_row, dt_row, norm_w)


R_EXP, R_GATE, R_RANK = 0, 4, 8


def _layer_norm(x, g, b):
    mu = jnp.mean(x, -1, keepdims=True)
    xc = x - mu
    var = jnp.mean(xc * xc, -1, keepdims=True)
    return xc * lax.rsqrt(var + LN_EPS) * g + b


def _oproj_body(oa_ref, ob_ref, x_ref, wa_ref, wb_ref, g_ref, b_ref, wr_ref, br_ref,
                h_ref, route_ref, cnt_ref, run_ref):
    i = pl.program_id(0)
    tm = x_ref.shape[0]

    @pl.when(i == 0)
    def _():
        run_ref[...] = jnp.zeros(run_ref.shape, F32)

    mix = _dot_bf(oa_ref[...], wa_ref[...]) + _dot_bf(ob_ref[...], wb_ref[...])
    h = _layer_norm(DEEPNORM_ALPHA * x_ref[...] + mix, g_ref[...], b_ref[...])
    h_ref[...] = h
    logits = _dot_f32(h, wr_ref[...]) + br_ref[...]
    lane = lax.broadcasted_iota(I32, (tm, LANE), 1)
    work = logits
    tops, idxs = [], []
    for _ in range(TOP_K):
        m = jnp.max(work, axis=1, keepdims=True)
        idx = jnp.min(jnp.where(work == m, lane, LANE), axis=1, keepdims=True)
        tops.append(m)
        idxs.append(idx)
        work = jnp.where(lane == idx, -jnp.inf, work)
    es = [jnp.exp(t - tops[0]) for t in tops]
    denom = es[0] + es[1] + es[2] + es[3]
    onehot = jnp.zeros((tm, LANE), F32)
    for idx in idxs:
        onehot = onehot + (lane == idx).astype(F32)
    r = lax.broadcasted_iota(I32, (tm, tm), 0)
    c = lax.broadcasted_iota(I32, (tm, tm), 1)
    before = jnp.dot((r > c).astype(BF16), onehot.astype(BF16), preferred_element_type=F32) + run_ref[...]
    rec = jnp.zeros((tm, LANE), F32)
    for k in range(TOP_K):
        rank = jnp.sum(jnp.where(lane == idxs[k], before, 0.0), axis=1, keepdims=True)
        rec = jnp.where(lane == R_EXP + k, idxs[k].astype(F32), rec)
        rec = jnp.where(lane == R_GATE + k, es[k] / denom, rec)
        rec = jnp.where(lane == R_RANK + k, rank, rec)
    route_ref[...] = rec
    run_ref[...] = run_ref[...] + jnp.sum(onehot, axis=0, keepdims=True)
    cnt_ref[...] = run_ref[...]


def _oproj_router(o_a, o_b, x, wa, wb, g1, b1, wr, br, tm):
    T = x.shape[0]
    row = lambda w: pl.BlockSpec((tm, w), lambda i: (i, 0))
    full = lambda a: pl.BlockSpec(a.shape, lambda i: (0,) * a.ndim)
    return pl.pallas_call(
        _oproj_body,
        grid=(T // tm,),
        in_specs=[row(o_a.shape[1]), row(o_b.shape[1]), row(D_MODEL),
                  full(wa), full(wb), full(g1), full(b1), full(wr), full(br)],
        out_specs=[row(D_MODEL), row(LANE), pl.BlockSpec((1, LANE), lambda i: (0, 0))],
        out_shape=[jax.ShapeDtypeStruct((T, D_MODEL), F32),
                   jax.ShapeDtypeStruct((T, LANE), F32),
                   jax.ShapeDtypeStruct((1, LANE), F32)],
        scratch_shapes=[pltpu.VMEM((1, LANE), F32)],
        compiler_params=_cparams(("arbitrary",)),
        name="out_proj_router",
    )(o_a, o_b, x, wa, wb, g1, b1, wr, br)


MOE_TM = 512
MOE_TF = 512
CMB_TN = 256


def _row_copies(tok_ref, h_hbm, xbuf_ref, sem, blk, slot, wait):
    tm = xbuf_ref.shape[1]
    base = blk * tm

    def body(r, carry):
        t = 0 if wait else tok_ref[base + r]
        cp = pltpu.make_async_copy(h_hbm.at[pl.ds(t, 1)], xbuf_ref.at[slot, pl.ds(r, 1)], sem.at[slot])
        cp.wait() if wait else cp.start()
        return carry

    lax.fori_loop(0, tm, body, 0, unroll=8)


def _ffn_body(be_ref, nu_ref, tok_ref, h_hbm, w1g_ref, w1l_ref, b1g_ref, b1l_ref, w2_ref, b2_ref, y_ref,
              xbuf_ref, xb_ref, sem):
    b = pl.program_id(0)
    f = pl.program_id(1)
    nu = nu_ref[0]
    first = f == 0

    @pl.when(first & (b == 0))
    def _():
        _row_copies(tok_ref, h_hbm, xbuf_ref, sem, 0, 0, False)

    @pl.when(first & (b + 1 < nu))
    def _():
        _row_copies(tok_ref, h_hbm, xbuf_ref, sem, b + 1, (b + 1) % 2, False)

    @pl.when(first & (b < nu))
    def _():
        _row_copies(tok_ref, h_hbm, xbuf_ref, sem, b, b % 2, True)
        xb_ref[...] = xbuf_ref[b % 2].astype(BF16)

    @pl.when(b < nu)
    def _():
        x = xb_ref[...]
        hg = jnp.dot(x, w1g_ref[0], preferred_element_type=F32) + b1g_ref[0]
        hl = jnp.dot(x, w1l_ref[0], preferred_element_type=F32) + b1l_ref[0]
        glu = jnp.minimum(hg, SWIGLU_LIMIT)
        lin = jnp.clip(hl, -SWIGLU_LIMIT, SWIGLU_LIMIT)
        act = glu * _sigmoid(SWIGLU_ALPHA * glu) * (lin + 1.0)
        part = jnp.dot(act.astype(BF16), w2_ref[0], preferred_element_type=F32)

        @pl.when(f == 0)
        def _():
            y_ref[...] = part + b2_ref[0]

        @pl.when(f > 0)
        def _():
            y_ref[...] = y_ref[...] + part

    @pl.when((b >= nu_ref[0]) & (f == 0))
    def _():
        y_ref[...] = jnp.zeros(y_ref.shape, F32)


def _moe_ffn(blk_e, n_used, row_tok, h, w1, b1, w2, b2, tm, tf):
    n_rows = row_tok.shape[0]
    D = h.shape[1]
    nf = D_FF // tf
    bb = lambda b, nu: jnp.minimum(b, nu[0] - 1)
    ff = lambda b, f, nu: jnp.where(b < nu[0], f, nf - 1)
    grid_spec = pltpu.PrefetchScalarGridSpec(
        num_scalar_prefetch=3,
        grid=(n_rows // tm, nf),
        in_specs=[pl.BlockSpec(memory_space=pl.ANY),
                  pl.BlockSpec((1, D, tf), lambda b, f, be, nu, tok: (be[bb(b, nu)], 0, ff(b, f, nu))),
                  pl.BlockSpec((1, D, tf), lambda b, f, be, nu, tok: (be[bb(b, nu)], 0, nf + ff(b, f, nu))),
                  pl.BlockSpec((1, 1, tf), lambda b, f, be, nu, tok: (be[bb(b, nu)], 0, ff(b, f, nu))),
                  pl.BlockSpec((1, 1, tf), lambda b, f, be, nu, tok: (be[bb(b, nu)], 0, nf + ff(b, f, nu))),
                  pl.BlockSpec((1, tf, D), lambda b, f, be, nu, tok: (be[bb(b, nu)], ff(b, f, nu), 0)),
                  pl.BlockSpec((1, 1, D), lambda b, f, be, nu, tok: (be[bb(b, nu)], 0, 0))],
        out_specs=pl.BlockSpec((tm, D), lambda b, f, be, nu, tok: (b, 0)),
        scratch_shapes=[pltpu.VMEM((2, tm, D), F32),
                        pltpu.VMEM((tm, D), BF16),
                        pltpu.SemaphoreType.DMA((2,))],
    )
    return pl.pallas_call(
        _ffn_body,
        grid_spec=grid_spec,
        out_shape=jax.ShapeDtypeStruct((n_rows, D), F32),
        compiler_params=_cparams(("arbitrary", "arbitrary")),
        name="moe_ffn",
    )(blk_e, n_used, row_tok, h, w1, w1, b1, b1, w2, b2)


def _combine_body(dest_ref, y_hbm, h_ref, route_ref, g_ref, b_ref, o_ref, buf_ref, sem):
    i = pl.program_id(0)
    tn = h_ref.shape[0]

    def issue(t, carry):
        for k in range(TOP_K):
            d = dest_ref[(i * tn + t) * TOP_K + k]
            pltpu.make_async_copy(y_hbm.at[pl.ds(d, 1)], buf_ref.at[k, pl.ds(t, 1)], sem).start()
        return carry

    lax.fori_loop(0, tn, issue, 0)

    def drain(t, carry):
        for k in range(TOP_K):
            pltpu.make_async_copy(y_hbm.at[pl.ds(0, 1)], buf_ref.at[k, pl.ds(t, 1)], sem).wait()
        return carry

    lax.fori_loop(0, tn, drain, 0)
    route = route_ref[...]
    f = route[:, R_GATE:R_GATE + 1] * buf_ref[0]
    for k in range(1, TOP_K):
        f = f + route[:, R_GATE + k:R_GATE + k + 1] * buf_ref[k]
    o_ref[...] = _layer_norm(DEEPNORM_ALPHA * h_ref[...] + f, g_ref[...], b_ref[...])


def _moe_combine(dest, y_rows, h, route, g2, b2, tn):
    T, D = h.shape
    grid_spec = pltpu.PrefetchScalarGridSpec(
        num_scalar_prefetch=1,
        grid=(T // tn,),
        in_specs=[pl.BlockSpec(memory_space=pl.ANY),
                  pl.BlockSpec((tn, D), lambda i, d: (i, 0)),
                  pl.BlockSpec((tn, LANE), lambda i, d: (i, 0)),
                  pl.BlockSpec((1, D), lambda i, d: (0, 0)),
                  pl.BlockSpec((1, D), lambda i, d: (0, 0))],
        out_specs=pl.BlockSpec((tn, D), lambda i, d: (i, 0)),
        scratch_shapes=[pltpu.VMEM((TOP_K, tn, D), F32), pltpu.SemaphoreType.DMA(())],
    )
    return pl.pallas_call(
        _combine_body,
        grid_spec=grid_spec,
        out_shape=jax.ShapeDtypeStruct((T, D), F32),
        compiler_params=_cparams(("arbitrary",)),
        name="moe_combine",
    )(dest, y_rows, h, route, g2, b2)


S_PG = 16
DEC_T = 8


def _page_copies(cache_hbm, buf_ref, sem, pt_ref, b, gi, slot, wait):
    spg = buf_ref.shape[1]
    for p in range(spg):
        src = cache_hbm.at[0] if wait else cache_hbm.at[pt_ref[b, gi * spg + p]]
        cp = pltpu.make_async_copy(src, buf_ref.at[slot, p], sem.at[slot])
        cp.wait() if wait else cp.start()


def _sidx_body(pt_ref, q_ref, w_ref, knew_ref, cache_hbm, o_ref, buf_ref, keys_ref, sem, *, topk, n_pages):
    b = pl.program_id(0)
    spg = buf_ref.shape[1]
    n_groups = n_pages // spg
    ncp = keys_ref.shape[0]
    q = q_ref[0]
    wl = jnp.broadcast_to(w_ref[0], (N_IDX_HEADS * DEC_T, LANE))

    def scores(kp):
        r = _dot_nt(q, kp)
        r = wl * jnp.maximum(r, 0.0)
        acc = r[0:DEC_T]
        for h in range(1, N_IDX_HEADS):
            acc = acc + r[h * DEC_T:(h + 1) * DEC_T]
        return acc

    _page_copies(cache_hbm, buf_ref, sem, pt_ref, b, 0, 0, False)

    def group(gi, carry):
        slot = gi % 2

        @pl.when(gi + 1 < n_groups)
        def _():
            _page_copies(cache_hbm, buf_ref, sem, pt_ref, b, gi + 1, 1 - slot, False)

        _page_copies(cache_hbm, buf_ref, sem, pt_ref, b, gi, slot, True)
        for p in range(spg):
            keys_ref[gi * spg + p] = _sortable(scores(buf_ref[slot, p]))
        return carry

    lax.fori_loop(0, n_groups, group, 0)
    row = lax.broadcasted_iota(I32, (DEC_T, LANE), 0)
    lane = lax.broadcasted_iota(I32, (DEC_T, LANE), 1)
    keys_ref[n_pages] = _sortable(jnp.where(lane <= row, scores(knew_ref[0]), -jnp.inf))
    for c in range(n_pages + 1, ncp):
        keys_ref[c] = jnp.full((DEC_T, LANE), INT_MIN, I32)

    U = 8

    def count(pred):
        def body(c, cnt):
            c0 = pl.multiple_of(c * U, U)
            kb = keys_ref[pl.ds(c0, U)]
            pos = (c0 + lax.broadcasted_iota(I32, (U, DEC_T, LANE), 0)) * LANE \
                + lax.broadcasted_iota(I32, (U, DEC_T, LANE), 2)
            return cnt + jnp.sum(pred(kb, pos).astype(I32), axis=0)

        cnt = lax.fori_loop(0, ncp // U, body, jnp.zeros((DEC_T, LANE), I32))
        return jnp.sum(cnt, axis=1, keepdims=True)

    thr = _kth_largest_key(lambda c: count(lambda kb, pos: kb >= c), (DEC_T, 1), topk)
    n_ge = count(lambda kb, pos: kb >= thr)
    n_gt = count(lambda kb, pos: kb > thr)
    need = jnp.where(thr == NEG_INF_KEY, 0, topk - n_gt)
    excess = jnp.max(n_ge - n_gt - need) > 0
    n_keys = (n_pages + 1) * LANE

    def tie_limit():
        nbits = max(1, int(math.ceil(math.log2(n_keys))))
        def bit_body(bi, p):
            cand = p + lax.shift_left(jnp.int32(1), nbits - 1 - bi)
            f = count(lambda kb, pos: (kb == thr) & (pos < cand))
            return jnp.where(f < need, cand, p)
        return lax.fori_loop(0, nbits, bit_body, jnp.zeros((DEC_T, 1), I32))

    lim = lax.cond(excess, tie_limit, lambda: jnp.full((DEC_T, 1), n_keys, I32))
    lim = jnp.where(need > 0, lim, -1)

    def write(c, carry):
        kb = keys_ref[c]
        pos = c * LANE + lane
        sel = (kb > thr) | ((kb == thr) & (pos <= lim))
        o_ref[0, c] = jnp.where(sel, 0.0, NEG)
        return carry

    lax.fori_loop(0, n_pages + 1, write, 0)


def _sample_indexer(page_table, q, w, k_new, cache_kidx, topk):
    DB, n_pages = page_table.shape
    ncp = (n_pages + 1 + 7) // 8 * 8
    grid_spec = pltpu.PrefetchScalarGridSpec(
        num_scalar_prefetch=1,
        grid=(DB,),
        in_specs=[pl.BlockSpec((1, N_IDX_HEADS * DEC_T, IDX_DIM), lambda b, pt: (b, 0, 0)),
                  pl.BlockSpec((1, N_IDX_HEADS * DEC_T, 1), lambda b, pt: (b, 0, 0)),
                  pl.BlockSpec((1, LANE, IDX_DIM), lambda b, pt: (b, 0, 0)),
                  pl.BlockSpec(memory_space=pl.ANY)],
        out_specs=pl.BlockSpec((1, n_pages + 1, DEC_T, LANE), lambda b, pt: (b, 0, 0, 0)),
        scratch_shapes=[pltpu.VMEM((2, _pick(n_pages, S_PG, 1), PAGE_SIZE, IDX_DIM), F32),
                        pltpu.VMEM((ncp, DEC_T, LANE), I32),
                        pltpu.SemaphoreType.DMA((2,))],
    )
    return pl.pallas_call(
        functools.partial(_sidx_body, topk=topk, n_pages=n_pages),
        grid_spec=grid_spec,
        out_shape=jax.ShapeDtypeStruct((DB, n_pages + 1, DEC_T, LANE), F32),
        compiler_params=_cparams(("arbitrary",)),
        name="sample_indexer",
    )(page_table, q, w, k_new, cache_kidx)


def _sattn_body(pt_ref, c31_ref, q_ref, knew_ref, vnew_ref, mask_ref, blast_ref, bnew_ref, ck_hbm, cv_hbm,
                o_ref, kbuf_ref, vbuf_ref, m_ref, l_ref, acc_ref, sem_k, sem_v, *, n_pages):
    b = pl.program_id(0)
    spg = kbuf_ref.shape[1]
    n_groups = n_pages // spg
    G = N_HEADS_A // N_KV_A
    m_ref[...] = jnp.full(m_ref.shape, NEG, F32)
    l_ref[...] = jnp.zeros(l_ref.shape, F32)
    acc_ref[...] = jnp.zeros(acc_ref.shape, F32)

    def flash(g, lg, vals):
        m_old = m_ref[g]
        m_new = jnp.maximum(m_old, jnp.max(lg, axis=1, keepdims=True))
        alpha = jnp.exp(m_old - m_new)
        p = jnp.exp(lg - m_new)
        l_ref[g] = alpha * l_ref[g] + jnp.sum(p, axis=1, keepdims=True)
        m_ref[g] = m_new
        acc_ref[g] = alpha * acc_ref[g] + _dot_bf(p, vals)

    def const_bias(g):
        rows = lax.broadcasted_iota(I32, (G * DEC_T, LANE), 0)
        out = jnp.full((G * DEC_T, LANE), c31_ref[g * G], F32)
        for r in range(1, G):
            out = jnp.where(rows >= r * DEC_T, c31_ref[g * G + r], out)
        return out

    _page_copies(ck_hbm, kbuf_ref, sem_k, pt_ref, b, 0, 0, False)
    _page_copies(cv_hbm, vbuf_ref, sem_v, pt_ref, b, 0, 0, False)

    def group(gi, carry):
        slot = gi % 2

        @pl.when(gi + 1 < n_groups)
        def _():
            _page_copies(ck_hbm, kbuf_ref, sem_k, pt_ref, b, gi + 1, 1 - slot, False)
            _page_copies(cv_hbm, vbuf_ref, sem_v, pt_ref, b, gi + 1, 1 - slot, False)

        _page_copies(ck_hbm, kbuf_ref, sem_k, pt_ref, b, gi, slot, True)
        _page_copies(cv_hbm, vbuf_ref, sem_v, pt_ref, b, gi, slot, True)
        mb = jnp.concatenate([mask_ref[0, gi * spg + p] for p in range(spg)], axis=1)
        mb = jnp.concatenate([mb] * G, axis=0)
        for g in range(N_KV_A):
            sl = slice(g * HEAD_DIM, (g + 1) * HEAD_DIM)
            cb = const_bias(g)
            last_b = jnp.where(gi == n_groups - 1, blast_ref[g], cb)
            parts = [_dot_nt(q_ref[0, g], kbuf_ref[slot, p, :, sl]) + (cb if p + 1 < spg else last_b)
                     for p in range(spg)]
            vals = vbuf_ref[slot, :, :, sl].reshape(spg * PAGE_SIZE, HEAD_DIM)
            flash(g, jnp.concatenate(parts, axis=1) + mb, vals)
        return carry

    lax.fori_loop(0, n_groups, group, 0)
    mb_new = jnp.concatenate([mask_ref[0, n_pages]] * G, axis=0)
    for g in range(N_KV_A):
        sl = slice(g * HEAD_DIM, (g + 1) * HEAD_DIM)
        flash(g, _dot_nt(q_ref[0, g], knew_ref[0, :, sl]) + bnew_ref[g] + mb_new, vnew_ref[0, :, sl])
    for g in range(N_KV_A):
        o_ref[0, g] = acc_ref[g] / l_ref[g]


def _sample_attention(page_table, c31, q, k_new, v_new, mask, b_last, b_new, cache_k, cache_v):
    DB, n_pages = page_table.shape
    G = N_HEADS_A // N_KV_A
    full = lambda a: pl.BlockSpec(a.shape, lambda b, pt, c: (0,) * a.ndim)
    grid_spec = pltpu.PrefetchScalarGridSpec(
        num_scalar_prefetch=2,
        grid=(DB,),
        in_specs=[pl.BlockSpec((1, N_KV_A, G * DEC_T, HEAD_DIM), lambda b, pt, c: (b, 0, 0, 0)),
                  pl.BlockSpec((1, LANE, KV_A), lambda b, pt, c: (b, 0, 0)),
                  pl.BlockSpec((1, LANE, KV_A), lambda b, pt, c: (b, 0, 0)),
                  pl.BlockSpec((1, n_pages + 1, DEC_T, LANE), lambda b, pt, c: (b, 0, 0, 0)),
                  full(b_last), full(b_new),
                  pl.BlockSpec(memory_space=pl.ANY), pl.BlockSpec(memory_space=pl.ANY)],
        out_specs=pl.BlockSpec((1, N_KV_A, G * DEC_T, HEAD_DIM), lambda b, pt, c: (b, 0, 0, 0)),
        scratch_shapes=[pltpu.VMEM((2, _pick(n_pages, S_PG, 1), PAGE_SIZE, KV_A), F32),
                        pltpu.VMEM((2, _pick(n_pages, S_PG, 1), PAGE_SIZE, KV_A), F32),
                        pltpu.VMEM((N_KV_A, G * DEC_T, 1), F32),
                        pltpu.VMEM((N_KV_A, G * DEC_T, 1), F32),
                        pltpu.VMEM((N_KV_A, G * DEC_T, HEAD_DIM), F32),
                        pltpu.SemaphoreType.DMA((2,)),
                        pltpu.SemaphoreType.DMA((2,))],
    )
    return pl.pallas_call(
        functools.partial(_sattn_body, n_pages=n_pages),
        grid_spec=grid_spec,
        out_shape=jax.ShapeDtypeStruct((DB, N_KV_A, G * DEC_T, HEAD_DIM), F32),
        compiler_params=_cparams(("arbitrary",)),
        name="sample_attention",
    )(page_table, c31, q, k_new, v_new, mask, b_last, b_new, cache_k, cache_v)


def _finish_layer(x, o_a, o_b, w_out, ln1_g, ln1_b, w_router, b_router, w1, b1, w2, b2, ln2_g, ln2_b):
    T = x.shape[0]
    E = w1.shape[0]
    half = o_a.shape[1]
    wa = w_out[:half].astype(BF16)
    wb = w_out[half:].astype(BF16)
    wr = jnp.zeros((D_MODEL, LANE), F32).at[:, :E].set(w_router.astype(F32))
    br = jnp.full((1, LANE), NEG, F32).at[0, :E].set(b_router.astype(F32))
    row2 = lambda a: a.astype(F32).reshape(1, -1)
    h, route, counts = _oproj_router(o_a, o_b, x, wa, wb, row2(ln1_g), row2(ln1_b), wr, br,
                                     tm=_pick(T, 256, SUBLANE))
    tm = MOE_TM
    top_e = route[:, R_EXP:R_EXP + TOP_K].astype(I32)
    rank = route[:, R_RANK:R_RANK + TOP_K].astype(I32)
    cnt = counts[0, :E].astype(I32)
    padded = (cnt + tm - 1) // tm * tm
    pend = jnp.cumsum(padded)
    pstart = pend - padded
    dest = (pstart[top_e] + rank).reshape(-1)
    n_blocks = (T * TOP_K + E * (tm - 1) + tm - 1) // tm
    tok = jnp.arange(T * TOP_K, dtype=I32) // TOP_K
    row_tok = jnp.zeros((n_blocks * tm,), I32).at[dest].set(tok)
    blk_e = jnp.minimum(jnp.searchsorted(pend, jnp.arange(n_blocks, dtype=I32) * tm, side='right'),
                        E - 1).astype(I32)
    n_used = (pend[-1:] // tm).astype(I32)
    y_rows = _moe_ffn(blk_e, n_used, row_tok, h, w1.astype(BF16), b1.astype(F32).reshape(E, 1, -1),
                      w2.astype(BF16), b2.astype(F32).reshape(E, 1, -1), tm, MOE_TF)
    return _moe_combine(dest, y_rows, h, route, row2(ln2_g), row2(ln2_b), _pick(T, CMB_TN, SUBLANE))


def _pick(n, cap, mult):
    best = None
    for d in range(mult, min(n, cap) + 1, mult):
        if n % d == 0:
            best = d
    assert best is not None, (n, cap, mult)
    return best


def kernel(x_prompt, x_sample, cache_k, cache_v, cache_kidx, state_conv, state_ssm, page_table, w_in, rel_table,
           w_out, conv_w, A_log, dt_bias, gdn_norm_w, ln1_g, ln1_b, w_router, b_router, w1, b1, w2, b2,
           ln2_g, ln2_b):
    BP, T, D = x_prompt.shape
    DB, TS, _ = x_sample.shape
    n_pool = cache_k.shape[0]
    n_pages = page_table.shape[1]
    past = n_pages * PAGE_SIZE
    assert BP == 1 and D == D_MODEL and TS == DEC_T and T % ATT_TK == 0
    G = N_HEADS_A // N_KV_A
    nq = T // LANE
    att_scale = HEAD_DIM ** -0.5
    idx_scale = IDX_DIM ** -0.5 * N_IDX_HEADS ** -0.5

    x_all = jnp.concatenate([x_prompt.reshape(T, D), x_sample.reshape(DB * TS, D)], axis=0).astype(F32)
    TA = x_all.shape[0]
    proj = _proj(x_all, _arrange_w_in(w_in), _pick(TA, 1280, SUBLANE), 768)
    bias = _bias_tiles(rel_table)
    c31 = rel_table[N_BUCKETS - 1].astype(F32)

    pp = proj[:T]
    k_p = pp[:, C_KA:C_KA + KV_A]
    v_p = pp[:, C_VA:C_VA + KV_A]
    ki_p = pp[:, C_MISC + M_KI:C_MISC + M_KI + IDX_DIM]
    mask_p = _prompt_indexer(proj, ki_p.astype(BF16), min(TOPK_MAX, T // 4))
    oa_p = _prompt_attention(proj, k_p.astype(BF16), v_p.T.astype(BF16), mask_p, bias)

    ps = proj[T:].reshape(DB, TS, N_PROJ)
    pad_rows = lambda a: jnp.pad(a, ((0, 0), (0, LANE - TS), (0, 0)))
    q_s = (ps[..., C_QA:C_QA + Q_A].reshape(DB, TS, N_KV_A, G, HEAD_DIM) * att_scale)
    q_s = q_s.transpose(0, 2, 3, 1, 4).reshape(DB, N_KV_A, G * TS, HEAD_DIM).astype(BF16)
    k_s = ps[..., C_KA:C_KA + KV_A]
    v_s = ps[..., C_VA:C_VA + KV_A]
    ki_s = ps[..., C_MISC + M_KI:C_MISC + M_KI + IDX_DIM]
    qi_s = ps[..., C_QI:C_QI + QI_W].reshape(DB, TS, N_IDX_HEADS, IDX_DIM)
    qi_s = qi_s.transpose(0, 2, 1, 3).reshape(DB, N_IDX_HEADS * TS, IDX_DIM).astype(BF16)
    w_s = (ps[..., C_MISC + M_WI:C_MISC + M_WI + N_IDX_HEADS] * idx_scale).transpose(0, 2, 1)
    w_s = w_s.reshape(DB, N_IDX_HEADS * TS, 1)
    mask_s = _sample_indexer(page_table, qi_s, w_s, pad_rows(ki_s), cache_kidx.astype(F32),
                             min(TOPK_MAX, (past + TS) // 4))
    to_rows = lambda t: t[:, :, :TS].transpose(0, 2, 1).reshape(N_KV_A, G * TS, LANE)
    o_s = _sample_attention(page_table, c31, q_s, pad_rows(k_s), pad_rows(v_s), mask_s,
                            to_rows(bias[1]), to_rows(bias[0]),
                            cache_k.astype(F32).reshape(n_pool, PAGE_SIZE, KV_A),
                            cache_v.astype(F32).reshape(n_pool, PAGE_SIZE, KV_A))
    oa_s = o_s.reshape(DB, N_KV_A, G, TS, HEAD_DIM).transpose(0, 3, 1, 2, 4).reshape(DB * TS, Q_A)

    lanes8 = lambda a: jnp.zeros((1, LANE), F32).at[0, M_A:M_A + N_HEADS_B].set(a.astype(F32))
    gdn_args = (conv_w.astype(F32), lanes8(A_log), lanes8(dt_bias), gdn_norm_w.astype(F32).reshape(1, HEAD_DIM))
    ob_p, cb_p, s_p = _gdn(proj, 0, BP, T, jnp.zeros((BP, SUBLANE, CONV_DIM), F32),
                           jnp.zeros((BP, N_HEADS_B, HEAD_DIM, HEAD_DIM), F32), *gdn_args)
    conv0_s = jnp.pad(state_conv.astype(F32), ((0, 0), (SUBLANE - (CONV_W - 1), 0), (0, 0)))
    ob_s, cb_s, s_s = _gdn(proj, T, DB, TS, conv0_s, state_ssm.astype(F32), *gdn_args)

    y_all = _finish_layer(x_all, jnp.concatenate([oa_p, oa_s], axis=0), jnp.concatenate([ob_p, ob_s], axis=0),
                          w_out, ln1_g, ln1_b, w_router, b_router, w1, b1, w2, b2, ln2_g, ln2_b)
    tail = SUBLANE - (CONV_W - 1)
    return (y_all[:T].reshape(BP, T, D), y_all[T:].reshape(DB, TS, D),
            k_p.reshape(BP, T, N_KV_A, HEAD_DIM), v_p.reshape(BP, T, N_KV_A, HEAD_DIM),
            ki_p.reshape(BP, T, IDX_DIM), cb_p[:, tail:], s_p,
            k_s.reshape(DB, TS, N_KV_A, HEAD_DIM), v_s.reshape(DB, TS, N_KV_A, HEAD_DIM), ki_s,
            cb_s[:, tail:], s_s)
```

```python
import functools
import math

import jax
import jax.numpy as jnp
import numpy as np
from jax import lax
from jax.experimental import pallas as pl
from jax.experimental.pallas import tpu as pltpu

F32 = jnp.float32
BF16 = jnp.bfloat16
I32 = jnp.int32

D_MODEL = 2048
HEAD_DIM = 128
N_HEADS_A = 8
N_KV_A = 4
N_IDX_HEADS = 16
IDX_DIM = 64
TOPK_MAX = 256
N_BUCKETS = 32
MAX_DISTANCE = 128
N_HEADS_B = 8
CONV_W = 4
CONV_DIM = N_HEADS_B * 3 * HEAD_DIM
TOP_K = 4
D_FF = D_MODEL
SWIGLU_LIMIT = 7.0
SWIGLU_ALPHA = 1.702
PAGE_SIZE = 128
DEPTH = 1
DEEPNORM_ALPHA = (2 * DEPTH) ** 0.25
LN_EPS = 1e-5

Q_A = N_HEADS_A * HEAD_DIM
KV_A = N_KV_A * HEAD_DIM
QI_W = N_IDX_HEADS * IDX_DIM
Z_B = N_HEADS_B * HEAD_DIM

LANE = 128
SUBLANE = 8
VMEM_LIMIT = 56 * 1024 * 1024

C_QA = 0
C_KA = C_QA + Q_A
C_VA = C_KA + KV_A
C_QI = C_VA + KV_A
C_QKV = C_QI + QI_W
C_Z = C_QKV + CONV_DIM
C_MISC = C_Z + Z_B
M_KI, M_WI, M_A, M_B = 0, 64, 80, 88
N_PROJ = 7680

NEG = -1e30
LOG2E = 1.4426950408889634
GDN_C = 128
GDN_HG = 4


def _cparams(sem, vmem=VMEM_LIMIT):
    return pltpu.CompilerParams(dimension_semantics=sem, vmem_limit_bytes=vmem)


def _proj_body(x_ref, w_ref, o_ref, xb_ref):
    @pl.when(pl.program_id(1) == 0)
    def _():
        xb_ref[...] = x_ref[...].astype(BF16)

    o_ref[...] = jnp.dot(xb_ref[...], w_ref[...], preferred_element_type=F32)


def _proj(x, w, tm, tn):
    T, D = x.shape
    N = w.shape[1]
    return pl.pallas_call(
        _proj_body,
        grid=(T // tm, N // tn),
        in_specs=[pl.BlockSpec((tm, D), lambda i, j: (i, 0)),
                  pl.BlockSpec((D, tn), lambda i, j: (0, j))],
        out_specs=pl.BlockSpec((tm, tn), lambda i, j: (i, j)),
        out_shape=jax.ShapeDtypeStruct((T, N), F32),
        scratch_shapes=[pltpu.VMEM((tm, D), BF16)],
        compiler_params=_cparams(("arbitrary", "arbitrary")),
        name="in_proj",
    )(x, w)


def _arrange_w_in(w_in):
    offs = np.cumsum([0, Q_A, KV_A, KV_A, QI_W, IDX_DIM, N_IDX_HEADS, CONV_DIM, Z_B, N_HEADS_B, N_HEADS_B])
    seg = lambda k: w_in[:, offs[k]:offs[k + 1]]
    qa, ka, va, qi, ki, wi, qkv, z, a, b = (seg(k) for k in range(10))
    pad_misc = jnp.zeros((w_in.shape[0], LANE - (IDX_DIM + N_IDX_HEADS + 2 * N_HEADS_B)), w_in.dtype)
    pad_tail = jnp.zeros((w_in.shape[0], N_PROJ - (C_MISC + LANE)), w_in.dtype)
    return jnp.concatenate([qa, ka, va, qi, qkv, z, ki, wi, a, b, pad_misc, pad_tail], axis=1).astype(BF16)


def _bias_body(rel_ref, o_ref, o2_ref):
    d = pl.program_id(0)
    s = lax.broadcasted_iota(I32, (LANE, LANE), 0)
    t = lax.broadcasted_iota(I32, (LANE, LANE), 1)
    n = jnp.maximum(d * LANE + t - s, 0)
    max_exact = N_BUCKETS // 2
    nf = jnp.maximum(n, 1).astype(F32)
    large = max_exact + (jnp.log(nf / max_exact) / math.log(MAX_DISTANCE / max_exact)
                         * (N_BUCKETS - max_exact)).astype(I32)
    large = jnp.minimum(large, N_BUCKETS - 1)
    bucket = jnp.where(n < max_exact, n, large)
    for h in range(N_HEADS_A):
        acc = jnp.zeros((LANE, LANE), F32)
        for b in range(N_BUCKETS):
            acc = jnp.where(bucket == b, rel_ref[b, h], acc)
        o_ref[0, h] = acc
        o2_ref[0, h] = acc * LOG2E


def _bias_tiles(rel_table):
    spec = pl.BlockSpec((1, N_HEADS_A, LANE, LANE), lambda d: (d, 0, 0, 0))
    shape = jax.ShapeDtypeStruct((3, N_HEADS_A, LANE, LANE), F32)
    return pl.pallas_call(
        _bias_body,
        grid=(3,),
        in_specs=[pl.BlockSpec(memory_space=pltpu.SMEM)],
        out_specs=[spec, spec],
        out_shape=[shape, shape],
        compiler_params=_cparams(("arbitrary",)),
        name="rel_bias_tiles",
    )(rel_table.astype(F32))


INT_MIN = -2 ** 31
NEG_INF_KEY = int(np.int32(np.array(-np.inf, np.float32).view(np.int32)) ^ np.int32(0x7FFFFFFF))


def _sortable(x):
    b = pltpu.bitcast(x, I32)
    return b ^ (lax.shift_right_arithmetic(b, 31) & jnp.int32(0x7FFFFFFF))


def _kth_largest_key(count_ge, shape, k):
    def bit_body(b, c):
        cand = c + lax.shift_left(jnp.int32(1), 31 - b)
        return jnp.where(count_ge(cand) >= k, cand, c)

    return lax.fori_loop(0, 32, bit_body, jnp.full(shape, INT_MIN, I32))


IDX_CH = 512


def _pidx_body(qi_ref, misc_ref, ki_ref, o_ref, keys_ref, qt_ref, w_ref, *, topk, n_keys):
    i = pl.program_id(0)
    nch = (i * LANE + LANE + IDX_CH - 1) // IDX_CH
    t_idx = i * LANE + lax.broadcasted_iota(I32, (IDX_CH, LANE), 1)
    s_loc = lax.broadcasted_iota(I32, (IDX_CH, LANE), 0)

    for p in range(N_IDX_HEADS // 2):
        xt = qi_ref[:, p * LANE:(p + 1) * LANE].T
        qt_ref[p] = jnp.concatenate([xt[:IDX_DIM], xt[IDX_DIM:]], axis=1).astype(BF16)
    w_ref[...] = misc_ref[...].T * (IDX_DIM ** -0.5 * N_IDX_HEADS ** -0.5)

    def score_chunk(c, carry):
        k0 = pl.multiple_of(c * IDX_CH, IDX_CH)
        kc = ki_ref[pl.ds(k0, IDX_CH), :]
        acc = jnp.zeros((IDX_CH, LANE), F32)
        for p in range(N_IDX_HEADS // 2):
            r = jnp.dot(kc, qt_ref[p], preferred_element_type=F32)
            acc = acc + w_ref[M_WI + 2 * p:M_WI + 2 * p + 1, :] * jnp.maximum(r[:, :LANE], 0.0)
            acc = acc + w_ref[M_WI + 2 * p + 1:M_WI + 2 * p + 2, :] * jnp.maximum(r[:, LANE:], 0.0)
        acc = jnp.where(k0 + s_loc <= t_idx, acc, -jnp.inf)
        keys_ref[pl.ds(k0, IDX_CH), :] = _sortable(acc)
        return carry

    lax.fori_loop(0, nch, score_chunk, 0)

    def count(pred):
        def body(c, cnt):
            k0 = pl.multiple_of(c * IDX_CH, IDX_CH)
            m = pred(keys_ref[pl.ds(k0, IDX_CH), :], k0).astype(I32)
            return cnt + jnp.sum(m.reshape(IDX_CH // SUBLANE, SUBLANE, LANE), axis=0)

        cnt = lax.fori_loop(0, nch, body, jnp.zeros((SUBLANE, LANE), I32))
        return jnp.sum(cnt, axis=0, keepdims=True)

    thr = _kth_largest_key(lambda c: count(lambda kb, k0: kb >= c), (1, LANE), topk)
    n_ge = count(lambda kb, k0: kb >= thr)
    n_gt = count(lambda kb, k0: kb > thr)
    need = jnp.where(thr == NEG_INF_KEY, 0, topk - n_gt)
    excess = jnp.max(n_ge - n_gt - need) > 0

    def tie_limit():
        nbits = max(1, int(math.ceil(math.log2(n_keys))))
        def bit_body(b, p):
            cand = p + lax.shift_left(jnp.int32(1), nbits - 1 - b)
            f = count(lambda kb, k0: (kb == thr) & (k0 + s_loc < cand))
            return jnp.where(f < need, cand, p)
        return lax.fori_loop(0, nbits, bit_body, jnp.zeros((1, LANE), I32))

    lim = lax.cond(excess, tie_limit, lambda: jnp.full((1, LANE), n_keys, I32))
    lim = jnp.where(need > 0, lim, -1)

    def write_chunk(c, carry):
        k0 = pl.multiple_of(c * IDX_CH, IDX_CH)
        kb = keys_ref[pl.ds(k0, IDX_CH), :]
        sel = (kb > thr) | ((kb == thr) & (k0 + s_loc <= lim))
        o_ref[pl.ds(k0, IDX_CH), :] = jnp.where(sel, 0.0, NEG).astype(o_ref.dtype)
        return carry

    lax.fori_loop(0, nch, write_chunk, 0)

    def fill_chunk(c, carry):
        k0 = pl.multiple_of(c * IDX_CH, IDX_CH)
        o_ref[pl.ds(k0, IDX_CH), :] = jnp.full((IDX_CH, LANE), NEG, o_ref.dtype)
        return carry

    lax.fori_loop(nch, n_keys // IDX_CH, fill_chunk, 0)


def _prompt_indexer(proj, ki, topk):
    T = ki.shape[0]
    return pl.pallas_call(
        functools.partial(_pidx_body, topk=topk, n_keys=T),
        grid=(T // LANE,),
        in_specs=[pl.BlockSpec((LANE, QI_W), lambda i: (i, C_QI // QI_W)),
                  pl.BlockSpec((LANE, LANE), lambda i: (i, C_MISC // LANE)),
                  pl.BlockSpec((T, IDX_DIM), lambda i: (0, 0))],
        out_specs=pl.BlockSpec((T, LANE), lambda i: (0, i)),
        out_shape=jax.ShapeDtypeStruct((T, T), BF16),
        scratch_shapes=[pltpu.VMEM((T, LANE), I32),
                        pltpu.VMEM((N_IDX_HEADS // 2, IDX_DIM, 2 * LANE), BF16),
                        pltpu.VMEM((LANE, LANE), F32)],
        compiler_params=_cparams(("arbitrary",)),
        name="prompt_indexer",
    )(proj, proj, ki)


ATT_TK = 512


def _pattn_body(ii_ref, jj_ref, q_ref, k_ref, vt_ref, mask_ref, bias_ref, o_ref, qt_ref, m_ref, l_ref, acc_ref,
                lg_ref, pb_ref):
    s = pl.program_id(0)
    i = ii_ref[s]
    j = jj_ref[s]
    G = N_HEADS_A // N_KV_A
    NU = ATT_TK // LANE

    @pl.when(j == 0)
    def _():
        m_ref[...] = jnp.full(m_ref.shape, NEG, F32)
        l_ref[...] = jnp.zeros(l_ref.shape, F32)
        acc_ref[...] = jnp.zeros(acc_ref.shape, F32)
        for h in range(N_HEADS_A):
            qh = q_ref[:, h * HEAD_DIM:(h + 1) * HEAD_DIM] * (HEAD_DIM ** -0.5 * LOG2E)
            qt_ref[h // G, :, (h % G) * LANE:(h % G + 1) * LANE] = qh.T.astype(BF16)

    bidx = [jnp.clip((i * LANE - j * ATT_TK - u * LANE) // LANE, 0, 2) for u in range(NU)]
    m_olds = [m_ref[g] for g in range(N_KV_A)]
    mxs = list(m_olds)
    for u in range(NU):
        rows = slice(u * LANE, (u + 1) * LANE)
        mbu = mask_ref[rows, :].astype(F32)
        for g in range(N_KV_A):
            st = jnp.dot(k_ref[rows, g * HEAD_DIM:(g + 1) * HEAD_DIM], qt_ref[g], preferred_element_type=F32)
            lgu = jnp.concatenate([st[:, r * LANE:(r + 1) * LANE] + (bias_ref[bidx[u], g * G + r] + mbu)
                                   for r in range(G)], axis=1)
            lg_ref[g, rows, :] = lgu
            mxs[g] = jnp.maximum(mxs[g], jnp.max(lgu, axis=0, keepdims=True))
    alphas = [jnp.exp2(m_olds[g] - mxs[g]) for g in range(N_KV_A)]
    lsums = [jnp.zeros_like(mxs[g]) for g in range(N_KV_A)]
    zero = lax.shift_right_arithmetic(j, 31) * LANE
    for u in range(NU):
        rows = slice(u * LANE, (u + 1) * LANE)
        for g in range(N_KV_A):
            p = jnp.exp2(lg_ref[g, pl.ds(pl.multiple_of(zero + u * LANE, LANE), LANE), :] - mxs[g])
            lsums[g] = lsums[g] + jnp.sum(p, axis=0, keepdims=True)
            pb_ref[g, rows, :] = p.astype(BF16)
    for g in range(N_KV_A):
        l_ref[g] = alphas[g] * l_ref[g] + lsums[g]
        m_ref[g] = mxs[g]
        vg = vt_ref[g * HEAD_DIM:(g + 1) * HEAD_DIM, :]
        acc_ref[g] = alphas[g] * acc_ref[g] + jnp.dot(vg, pb_ref[g], preferred_element_type=F32)

    @pl.when((j + 1) * ATT_TK > i * LANE + LANE - 1)
    def _():
        for h in range(N_HEADS_A):
            cs = slice((h % G) * LANE, (h % G + 1) * LANE)
            o_ref[:, h * HEAD_DIM:(h + 1) * HEAD_DIM] = (acc_ref[h // G][:, cs] / l_ref[h // G][:, cs]).T


def _prompt_attention(proj, k, v_t, mask, bias):
    T = k.shape[0]
    nq = T // LANE
    ii, jj = [], []
    for i in range(nq):
        for j in range((i * LANE + LANE - 1) // ATT_TK + 1):
            ii.append(i)
            jj.append(j)
    ii = jnp.asarray(np.array(ii, np.int32))
    jj = jnp.asarray(np.array(jj, np.int32))
    G = N_HEADS_A // N_KV_A
    grid_spec = pltpu.PrefetchScalarGridSpec(
        num_scalar_prefetch=2,
        grid=(int(ii.shape[0]),),
        in_specs=[pl.BlockSpec((LANE, Q_A), lambda s, ii, jj: (ii[s], C_QA // Q_A)),
                  pl.BlockSpec((ATT_TK, KV_A), lambda s, ii, jj: (jj[s], 0)),
                  pl.BlockSpec((KV_A, ATT_TK), lambda s, ii, jj: (0, jj[s])),
                  pl.BlockSpec((ATT_TK, LANE), lambda s, ii, jj: (jj[s], ii[s])),
                  pl.BlockSpec((3, N_HEADS_A, LANE, LANE), lambda s, ii, jj: (0, 0, 0, 0))],
        out_specs=pl.BlockSpec((LANE, Q_A), lambda s, ii, jj: (ii[s], 0)),
        scratch_shapes=[pltpu.VMEM((N_KV_A, HEAD_DIM, G * LANE), BF16),
                        pltpu.VMEM((N_KV_A, 1, G * LANE), F32),
                        pltpu.VMEM((N_KV_A, 1, G * LANE), F32),
                        pltpu.VMEM((N_KV_A, HEAD_DIM, G * LANE), F32),
                        pltpu.VMEM((N_KV_A, ATT_TK, G * LANE), F32),
                        pltpu.VMEM((N_KV_A, ATT_TK, G * LANE), BF16)],
    )
    return pl.pallas_call(
        _pattn_body,
        grid_spec=grid_spec,
        out_shape=jax.ShapeDtypeStruct((T, Q_A), F32),
        compiler_params=_cparams(("arbitrary",)),
        name="prompt_attention",
    )(ii, jj, proj, k, v_t, mask, bias)


def _split3(a):
    a1 = a.astype(BF16)
    r = a - a1.astype(F32)
    a2 = r.astype(BF16)
    a3 = (r - a2.astype(F32)).astype(BF16)
    return a1, a2, a3


def _dot_f32(a, b):
    a1, a2, a3 = _split3(a)
    b1, b2, b3 = _split3(b)
    d = lambda p, q: jnp.dot(p, q, preferred_element_type=F32)
    return (d(a1, b1) + (d(a1, b2) + d(a2, b1))) + ((d(a1, b3) + d(a3, b1)) + d(a2, b2))


def _split2(a):
    a1 = a.astype(BF16)
    return a1, (a - a1.astype(F32)).astype(BF16)


def _dot_3p(a, b):
    a1, a2 = _split2(a)
    b1, b2 = _split2(b)
    d = lambda p, q: jnp.dot(p, q, preferred_element_type=F32)
    return d(a1, b1) + (d(a1, b2) + d(a2, b1))


def _dot_bf(a, b):
    return jnp.dot(a.astype(BF16), b.astype(BF16), preferred_element_type=F32)


def _dot_nt(a, b):
    return lax.dot_general(a.astype(BF16), b.astype(BF16), (((1,), (1,)), ((), ())), preferred_element_type=F32)


def _sigmoid(x):
    return 1.0 / (1.0 + jnp.exp(-x))


def _softplus(x):
    return jnp.maximum(x, 0.0) + jnp.log(1.0 + jnp.exp(-jnp.abs(x)))


def _gdn_body(qkv_ref, z_ref, misc_ref, conv0_ref, s0_ref, cw_ref, arow_ref, dtrow_ref, nw_ref,
              o_ref, convo_ref, so_ref, xbuf_ref, s_ref, *, n_valid):
    C = GDN_C
    n = pl.program_id(1)
    H = N_HEADS_B

    @pl.when(n == 0)
    def _():
        xbuf_ref[0:SUBLANE, :] = conv0_ref[0]
        s_ref[...] = s0_ref[0]
        if n_valid < C:
            xbuf_ref[SUBLANE:, :] = jnp.zeros((C, CONV_DIM), F32)

    xbuf_ref[SUBLANE:SUBLANE + n_valid, :] = qkv_ref[...]
    conv = xbuf_ref[5:5 + C, :] * cw_ref[0:1, :]
    for i in range(1, CONV_W):
        conv = conv + xbuf_ref[5 + i:5 + i + C, :] * cw_ref[i:i + 1, :]
    tail = xbuf_ref[n_valid:n_valid + SUBLANE, :]
    xbuf_ref[0:SUBLANE, :] = tail
    convo_ref[0] = tail

    row = lax.broadcasted_iota(I32, (C, LANE), 0)
    col = lax.broadcasted_iota(I32, (C, LANE), 1)
    live = row < n_valid
    hact = conv * _sigmoid(conv)

    misc = misc_ref[...]
    if n_valid < C:
        misc = jnp.concatenate([misc, jnp.zeros((C - n_valid, LANE), F32)], axis=0)
    g_full = jnp.where(live, -jnp.exp(arow_ref[...]) * _softplus(misc + dtrow_ref[...]), 0.0)
    beta_full = jnp.where(live, _sigmoid(misc), 0.0)
    tril = (row >= col).astype(F32)
    gc_full = _dot_f32(tril, g_full)
    gc_t = gc_full.T
    strict = row > col
    incl = row >= col

    def setup(hd):
        sl = slice(hd * HEAD_DIM, (hd + 1) * HEAD_DIM)
        q = hact[:, sl]
        k = hact[:, H * HEAD_DIM + hd * HEAD_DIM:H * HEAD_DIM + (hd + 1) * HEAD_DIM]
        v = hact[:, 2 * H * HEAD_DIM + hd * HEAD_DIM:2 * H * HEAD_DIM + (hd + 1) * HEAD_DIM]
        q = q * lax.rsqrt(jnp.sum(q * q, -1, keepdims=True) + 1e-6) * HEAD_DIM ** -0.5
        k = k * lax.rsqrt(jnp.sum(k * k, -1, keepdims=True) + 1e-6)
        q = jnp.where(live, q, 0.0)
        k = jnp.where(live, k, 0.0)
        gc_col = jnp.broadcast_to(gc_full[:, M_A + hd:M_A + hd + 1], (C, LANE))
        gc_row = jnp.broadcast_to(gc_t[M_A + hd:M_A + hd + 1, :], (C, LANE))
        gc_last = jnp.broadcast_to(gc_full[C - 1:C, M_A + hd:M_A + hd + 1], (C, LANE))
        beta = jnp.broadcast_to(beta_full[:, M_B + hd:M_B + hd + 1], (C, LANE))
        decay = jnp.where(incl, jnp.exp(jnp.where(incl, gc_col - gc_row, 0.0)), 0.0)
        kk = _dot_nt(k, k)
        qk = _dot_nt(q, k) * decay
        e_gc = jnp.exp(gc_col)
        m = jnp.where(strict, -(beta * decay * kk), 0.0)
        y = jnp.concatenate([v * beta, k * (beta * e_gc)], axis=1)
        return dict(q=q, k=k, qk=qk, e_gc=e_gc, gc_col=gc_col, gc_last=gc_last, m=m, y=y)

    def finish(hd, st):
        sl = slice(hd * HEAD_DIM, (hd + 1) * HEAD_DIM)
        u0 = st["y"][:, :HEAD_DIM]
        w = st["y"][:, HEAD_DIM:]
        s_old = s_ref[hd]
        u = u0 - _dot_bf(w, s_old)
        o = _dot_bf(st["q"] * st["e_gc"], s_old) + _dot_bf(st["qk"], u)
        kd = st["k"] * jnp.exp(st["gc_last"] - st["gc_col"])
        s_ref[hd] = jnp.exp(st["gc_last"]) * s_old + _dot_bf(kd.T, u)
        o = o * lax.rsqrt(jnp.mean(o * o, -1, keepdims=True) + 1e-6) * nw_ref[...]
        zz = z_ref[:, sl]
        o_ref[:, sl] = o[:n_valid] * (zz * _sigmoid(zz))

    nlev = int(math.log2(C))
    for h0 in range(0, H, GDN_HG):
        sts = [setup(hd) for hd in range(h0, h0 + GDN_HG)]
        for lvl in range(nlev):
            for st in sts:
                if lvl + 1 < nlev:
                    mp = _dot_3p(st["m"], jnp.concatenate([st["m"], st["y"]], axis=1))
                    st["y"] = st["y"] + mp[:, C:]
                    st["m"] = mp[:, :C]
                else:
                    st["y"] = st["y"] + _dot_3p(st["m"], st["y"])
        for i, st in enumerate(sts):
            finish(h0 + i, st)

    so_ref[0] = s_ref[...]


def _gdn(proj, row0, B, T, conv0, s0, conv_w, a_row, dt_row, norm_w):
    n_valid = min(T, GDN_C)
    nblk = T // n_valid
    r0 = row0 // n_valid
    rb = lambda b, n: r0 + b * nblk + n
    return pl.pallas_call(
        functools.partial(_gdn_body, n_valid=n_valid),
        grid=(B, nblk),
        in_specs=[pl.BlockSpec((n_valid, CONV_DIM), lambda b, n: (rb(b, n), C_QKV // CONV_DIM)),
                  pl.BlockSpec((n_valid, Z_B), lambda b, n: (rb(b, n), C_Z // Z_B)),
                  pl.BlockSpec((n_valid, LANE), lambda b, n: (rb(b, n), C_MISC // LANE)),
                  pl.BlockSpec((1, SUBLANE, CONV_DIM), lambda b, n: (b, 0, 0)),
                  pl.BlockSpec((1, N_HEADS_B, HEAD_DIM, HEAD_DIM), lambda b, n: (b, 0, 0, 0)),
                  pl.BlockSpec((CONV_W, CONV_DIM), lambda b, n: (0, 0)),
                  pl.BlockSpec((1, LANE), lambda b, n: (0, 0)),
                  pl.BlockSpec((1, LANE), lambda b, n: (0, 0)),
                  pl.BlockSpec((1, HEAD_DIM), lambda b, n: (0, 0))],
        out_specs=[pl.BlockSpec((n_valid, Z_B), lambda b, n: (b * nblk + n, 0)),
                   pl.BlockSpec((1, SUBLANE, CONV_DIM), lambda b, n: (b, 0, 0)),
                   pl.BlockSpec((1, N_HEADS_B, HEAD_DIM, HEAD_DIM), lambda b, n: (b, 0, 0, 0))],
        out_shape=[jax.ShapeDtypeStruct((B * T, Z_B), F32),
                   jax.ShapeDtypeStruct((B, SUBLANE, CONV_DIM), F32),
                   jax.ShapeDtypeStruct((B, N_HEADS_B, HEAD_DIM, HEAD_DIM), F32)],
        scratch_shapes=[pltpu.VMEM((SUBLANE + GDN_C, CONV_DIM), F32),
                        pltpu.VMEM((N_HEADS_B, HEAD_DIM, HEAD_DIM), F32)],
        compiler_params=_cparams(("arbitrary", "arbitrary")),
        name="gdn_mixer",
    )(proj, proj, proj, conv0, s0, conv_w, a_row, dt_row, norm_w)


R_EXP, R_GATE, R_RANK = 0, 4, 8


def _layer_norm(x, g, b):
    mu = jnp.mean(x, -1, keepdims=True)
    xc = x - mu
    var = jnp.mean(xc * xc, -1, keepdims=True)
    return xc * lax.rsqrt(var + LN_EPS) * g + b


def _oproj_body(oa_ref, ob_ref, x_ref, wa_ref, wb_ref, g_ref, b_ref, wr_ref, br_ref,
                h_ref, route_ref, cnt_ref, run_ref):
    i = pl.program_id(0)
    tm = x_ref.shape[0]

    @pl.when(i == 0)
    def _():
        run_ref[...] = jnp.zeros(run_ref.shape, F32)

    mix = _dot_bf(oa_ref[...], wa_ref[...]) + _dot_bf(ob_ref[...], wb_ref[...])
    h = _layer_norm(DEEPNORM_ALPHA * x_ref[...] + mix, g_ref[...], b_ref[...])
    h_ref[...] = h
    logits = _dot_f32(h, wr_ref[...]) + br_ref[...]
    lane = lax.broadcasted_iota(I32, (tm, LANE), 1)
    work = logits
    tops, idxs = [], []
    for _ in range(TOP_K):
        m = jnp.max(work, axis=1, keepdims=True)
        idx = jnp.min(jnp.where(work == m, lane, LANE), axis=1, keepdims=True)
        tops.append(m)
        idxs.append(idx)
        work = jnp.where(lane == idx, -jnp.inf, work)
    es = [jnp.exp(t - tops[0]) for t in tops]
    denom = es[0] + es[1] + es[2] + es[3]
    onehot = jnp.zeros((tm, LANE), F32)
    for idx in idxs:
        onehot = onehot + (lane == idx).astype(F32)
    r = lax.broadcasted_iota(I32, (tm, tm), 0)
    c = lax.broadcasted_iota(I32, (tm, tm), 1)
    before = jnp.dot((r > c).astype(BF16), onehot.astype(BF16), preferred_element_type=F32) + run_ref[...]
    rec = jnp.zeros((tm, LANE), F32)
    for k in range(TOP_K):
        rank = jnp.sum(jnp.where(lane == idxs[k], before, 0.0), axis=1, keepdims=True)
        rec = jnp.where(lane == R_EXP + k, idxs[k].astype(F32), rec)
        rec = jnp.where(lane == R_GATE + k, es[k] / denom, rec)
        rec = jnp.where(lane == R_RANK + k, rank, rec)
    route_ref[...] = rec
    run_ref[...] = run_ref[...] + jnp.sum(onehot, axis=0, keepdims=True)
    cnt_ref[...] = run_ref[...]


def _oproj_router(o_a, o_b, x, wa, wb, g1, b1, wr, br, tm):
    T = x.shape[0]
    row = lambda w: pl.BlockSpec((tm, w), lambda i: (i, 0))
    full = lambda a: pl.BlockSpec(a.shape, lambda i: (0,) * a.ndim)
    return pl.pallas_call(
        _oproj_body,
        grid=(T // tm,),
        in_specs=[row(o_a.shape[1]), row(o_b.shape[1]), row(D_MODEL),
                  full(wa), full(wb), full(g1), full(b1), full(wr), full(br)],
        out_specs=[row(D_MODEL), row(LANE), pl.BlockSpec((1, LANE), lambda i: (0, 0))],
        out_shape=[jax.ShapeDtypeStruct((T, D_MODEL), F32),
                   jax.ShapeDtypeStruct((T, LANE), F32),
                   jax.ShapeDtypeStruct((1, LANE), F32)],
        scratch_shapes=[pltpu.VMEM((1, LANE), F32)],
        compiler_params=_cparams(("arbitrary",)),
        name="out_proj_router",
    )(o_a, o_b, x, wa, wb, g1, b1, wr, br)


MOE_TM = 512
MOE_TF = 512
CMB_TN = 256


def _row_copies(tok_ref, h_hbm, xbuf_ref, sem, blk, slot, wait):
    tm = xbuf_ref.shape[1]
    base = blk * tm

    def body(r, carry):
        t = 0 if wait else tok_ref[base + r]
        cp = pltpu.make_async_copy(h_hbm.at[pl.ds(t, 1)], xbuf_ref.at[slot, pl.ds(r, 1)], sem.at[slot])
        cp.wait() if wait else cp.start()
        return carry

    lax.fori_loop(0, tm, body, 0, unroll=8)


def _ffn_body(be_ref, nu_ref, tok_ref, h_hbm, w1g_ref, w1l_ref, b1g_ref, b1l_ref, w2_ref, b2_ref, y_ref,
              xbuf_ref, xb_ref, sem):
    b = pl.program_id(0)
    f = pl.program_id(1)
    nu = nu_ref[0]
    first = f == 0
    tm = xbuf_ref.shape[1]
    tq = tm // (D_FF // w1g_ref.shape[2])

    @pl.when(first & (b == 0))
    def _():
        _row_copies(tok_ref, h_hbm, xbuf_ref, sem, 0, 0, False)

    @pl.when(first & (b < nu))
    def _():
        _row_copies(tok_ref, h_hbm, xbuf_ref, sem, b, b % 2, True)
        xb_ref[...] = xbuf_ref[b % 2].astype(BF16)

    @pl.when(first & (b == nu))
    def _():
        _row_copies(tok_ref, h_hbm, xbuf_ref, sem, 0, (nu % 2), True)

    @pl.when(b < nu)
    def _():
        for r in range(tq):
            row = f * tq + r
            pltpu.make_async_copy(h_hbm.at[pl.ds(tok_ref[(b + 1) * tm + row], 1)],
                                  xbuf_ref.at[(b + 1) % 2, pl.ds(row, 1)], sem.at[(b + 1) % 2]).start()
        x = xb_ref[...]
        hg = jnp.dot(x, w1g_ref[0], preferred_element_type=F32) + b1g_ref[0]
        hl = jnp.dot(x, w1l_ref[0], preferred_element_type=F32) + b1l_ref[0]
        glu = jnp.minimum(hg, SWIGLU_LIMIT)
        lin = jnp.clip(hl, -SWIGLU_LIMIT, SWIGLU_LIMIT)
        act = glu * _sigmoid(SWIGLU_ALPHA * glu) * (lin + 1.0)
        part = jnp.dot(act.astype(BF16), w2_ref[0], preferred_element_type=F32)

        @pl.when(f == 0)
        def _():
            y_ref[...] = part + b2_ref[0]

        @pl.when(f > 0)
        def _():
            y_ref[...] = y_ref[...] + part

    @pl.when((b >= nu_ref[0]) & (f == 0))
    def _():
        y_ref[...] = jnp.zeros(y_ref.shape, F32)


def _moe_ffn(blk_e, n_used, row_tok, h, w1, b1, w2, b2, tm, tf):
    n_rows = row_tok.shape[0]
    D = h.shape[1]
    nf = D_FF // tf
    bb = lambda b, nu: jnp.minimum(b, nu[0] - 1)
    ff = lambda b, f, nu: jnp.where(b < nu[0], f, nf - 1)
    grid_spec = pltpu.PrefetchScalarGridSpec(
        num_scalar_prefetch=3,
        grid=(n_rows // tm, nf),
        in_specs=[pl.BlockSpec(memory_space=pl.ANY),
                  pl.BlockSpec((1, D, tf), lambda b, f, be, nu, tok: (be[bb(b, nu)], 0, ff(b, f, nu))),
                  pl.BlockSpec((1, D, tf), lambda b, f, be, nu, tok: (be[bb(b, nu)], 0, nf + ff(b, f, nu))),
                  pl.BlockSpec((1, 1, tf), lambda b, f, be, nu, tok: (be[bb(b, nu)], 0, ff(b, f, nu))),
                  pl.BlockSpec((1, 1, tf), lambda b, f, be, nu, tok: (be[bb(b, nu)], 0, nf + ff(b, f, nu))),
                  pl.BlockSpec((1, tf, D), lambda b, f, be, nu, tok: (be[bb(b, nu)], ff(b, f, nu), 0)),
                  pl.BlockSpec((1, 1, D), lambda b, f, be, nu, tok: (be[bb(b, nu)], 0, 0))],
        out_specs=pl.BlockSpec((tm, D), lambda b, f, be, nu, tok: (b, 0)),
        scratch_shapes=[pltpu.VMEM((2, tm, D), F32),
                        pltpu.VMEM((tm, D), BF16),
                        pltpu.SemaphoreType.DMA((2,))],
    )
    return pl.pallas_call(
        _ffn_body,
        grid_spec=grid_spec,
        out_shape=jax.ShapeDtypeStruct((n_rows, D), F32),
        compiler_params=_cparams(("arbitrary", "arbitrary")),
        name="moe_ffn",
    )(blk_e, n_used, row_tok, h, w1, w1, b1, b1, w2, b2)


def _combine_body(dest_ref, y_hbm, h_ref, route_ref, g_ref, b_ref, o_ref, buf_ref, sem):
    i = pl.program_id(0)
    tn = h_ref.shape[0]

    def issue(t, carry):
        for k in range(TOP_K):
            d = dest_ref[(i * tn + t) * TOP_K + k]
            pltpu.make_async_copy(y_hbm.at[pl.ds(d, 1)], buf_ref.at[k, pl.ds(t, 1)], sem).start()
        return carry

    lax.fori_loop(0, tn, issue, 0)

    def drain(t, carry):
        for k in range(TOP_K):
            pltpu.make_async_copy(y_hbm.at[pl.ds(0, 1)], buf_ref.at[k, pl.ds(t, 1)], sem).wait()
        return carry

    lax.fori_loop(0, tn, drain, 0)
    route = route_ref[...]
    f = route[:, R_GATE:R_GATE + 1] * buf_ref[0]
    for k in range(1, TOP_K):
        f = f + route[:, R_GATE + k:R_GATE + k + 1] * buf_ref[k]
    o_ref[...] = _layer_norm(DEEPNORM_ALPHA * h_ref[...] + f, g_ref[...], b_ref[...])


def _moe_combine(dest, y_rows, h, route, g2, b2, tn):
    T, D = h.shape
    grid_spec = pltpu.PrefetchScalarGridSpec(
        num_scalar_prefetch=1,
        grid=(T // tn,),
        in_specs=[pl.BlockSpec(memory_space=pl.ANY),
                  pl.BlockSpec((tn, D), lambda i, d: (i, 0)),
                  pl.BlockSpec((tn, LANE), lambda i, d: (i, 0)),
                  pl.BlockSpec((1, D), lambda i, d: (0, 0)),
                  pl.BlockSpec((1, D), lambda i, d: (0, 0))],
        out_specs=pl.BlockSpec((tn, D), lambda i, d: (i, 0)),
        scratch_shapes=[pltpu.VMEM((TOP_K, tn, D), F32), pltpu.SemaphoreType.DMA(())],
    )
    return pl.pallas_call(
        _combine_body,
        grid_spec=grid_spec,
        out_shape=jax.ShapeDtypeStruct((T, D), F32),
        compiler_params=_cparams(("arbitrary",)),
        name="moe_combine",
    )(dest, y_rows, h, route, g2, b2)


S_PG = 16
DEC_T = 8


def _page_copies(cache_hbm, buf_ref, sem, pt_ref, b, gi, slot, wait):
    spg = buf_ref.shape[1]
    for p in range(spg):
        src = cache_hbm.at[0] if wait else cache_hbm.at[pt_ref[b, gi * spg + p]]
        cp = pltpu.make_async_copy(src, buf_ref.at[slot, p], sem.at[slot])
        cp.wait() if wait else cp.start()


def _sidx_body(pt_ref, q_ref, w_ref, knew_ref, cache_hbm, o_ref, buf_ref, keys_ref, sem, *, topk, n_pages):
    b = pl.program_id(0)
    spg = buf_ref.shape[1]
    n_groups = n_pages // spg
    ncp = keys_ref.shape[0]
    q = q_ref[0]
    wl = jnp.broadcast_to(w_ref[0], (N_IDX_HEADS * DEC_T, LANE))

    def scores(kp):
        r = _dot_nt(q, kp)
        r = wl * jnp.maximum(r, 0.0)
        acc = r[0:DEC_T]
        for h in range(1, N_IDX_HEADS):
            acc = acc + r[h * DEC_T:(h + 1) * DEC_T]
        return acc

    _page_copies(cache_hbm, buf_ref, sem, pt_ref, b, 0, 0, False)

    def group(gi, carry):
        slot = gi % 2

        @pl.when(gi + 1 < n_groups)
        def _():
            _page_copies(cache_hbm, buf_ref, sem, pt_ref, b, gi + 1, 1 - slot, False)

        _page_copies(cache_hbm, buf_ref, sem, pt_ref, b, gi, slot, True)
        for p in range(spg):
            keys_ref[gi * spg + p] = _sortable(scores(buf_ref[slot, p]))
        return carry

    lax.fori_loop(0, n_groups, group, 0)
    row = lax.broadcasted_iota(I32, (DEC_T, LANE), 0)
    lane = lax.broadcasted_iota(I32, (DEC_T, LANE), 1)
    keys_ref[n_pages] = _sortable(jnp.where(lane <= row, scores(knew_ref[0]), -jnp.inf))
    for c in range(n_pages + 1, ncp):
        keys_ref[c] = jnp.full((DEC_T, LANE), INT_MIN, I32)

    U = 8

    def count(pred):
        def body(c, cnt):
            c0 = pl.multiple_of(c * U, U)
            kb = keys_ref[pl.ds(c0, U)]
            pos = (c0 + lax.broadcasted_iota(I32, (U, DEC_T, LANE), 0)) * LANE \
                + lax.broadcasted_iota(I32, (U, DEC_T, LANE), 2)
            return cnt + jnp.sum(pred(kb, pos).astype(I32), axis=0)

        cnt = lax.fori_loop(0, ncp // U, body, jnp.zeros((DEC_T, LANE), I32))
        return jnp.sum(cnt, axis=1, keepdims=True)

    thr = _kth_largest_key(lambda c: count(lambda kb, pos: kb >= c), (DEC_T, 1), topk)
    n_ge = count(lambda kb, pos: kb >= thr)
    n_gt = count(lambda kb, pos: kb > thr)
    need = jnp.where(thr == NEG_INF_KEY, 0, topk - n_gt)
    excess = jnp.max(n_ge - n_gt - need) > 0
    n_keys = (n_pages + 1) * LANE

    def tie_limit():
        nbits = max(1, int(math.ceil(math.log2(n_keys))))
        def bit_body(bi, p):
            cand = p + lax.shift_left(jnp.int32(1), nbits - 1 - bi)
            f = count(lambda kb, pos: (kb == thr) & (pos < cand))
            return jnp.where(f < need, cand, p)
        return lax.fori_loop(0, nbits, bit_body, jnp.zeros((DEC_T, 1), I32))

    lim = lax.cond(excess, tie_limit, lambda: jnp.full((DEC_T, 1), n_keys, I32))
    lim = jnp.where(need > 0, lim, -1)

    def write(c, carry):
        kb = keys_ref[c]
        pos = c * LANE + lane
        sel = (kb > thr) | ((kb == thr) & (pos <= lim))
        o_ref[0, c] = jnp.where(sel, 0.0, NEG)
        return carry

    lax.fori_loop(0, n_pages + 1, write, 0)


def _sample_indexer(page_table, q, w, k_new, cache_kidx, topk):
    DB, n_pages = page_table.shape
    ncp = (n_pages + 1 + 7) // 8 * 8
    grid_spec = pltpu.PrefetchScalarGridSpec(
        num_scalar_prefetch=1,
        grid=(DB,),
        in_specs=[pl.BlockSpec((1, N_IDX_HEADS * DEC_T, IDX_DIM), lambda b, pt: (b, 0, 0)),
                  pl.BlockSpec((1, N_IDX_HEADS * DEC_T, 1), lambda b, pt: (b, 0, 0)),
                  pl.BlockSpec((1, LANE, IDX_DIM), lambda b, pt: (b, 0, 0)),
                  pl.BlockSpec(memory_space=pl.ANY)],
        out_specs=pl.BlockSpec((1, n_pages + 1, DEC_T, LANE), lambda b, pt: (b, 0, 0, 0)),
        scratch_shapes=[pltpu.VMEM((2, _pick(n_pages, S_PG, 1), PAGE_SIZE, IDX_DIM), F32),
                        pltpu.VMEM((ncp, DEC_T, LANE), I32),
                        pltpu.SemaphoreType.DMA((2,))],
    )
    return pl.pallas_call(
        functools.partial(_sidx_body, topk=topk, n_pages=n_pages),
        grid_spec=grid_spec,
        out_shape=jax.ShapeDtypeStruct((DB, n_pages + 1, DEC_T, LANE), F32),
        compiler_params=_cparams(("arbitrary",)),
        name="sample_indexer",
    )(page_table, q, w, k_new, cache_kidx)


def _sattn_body(pt_ref, c31_ref, q_ref, knew_ref, vnew_ref, mask_ref, blast_ref, bnew_ref, ck_hbm, cv_hbm,
                o_ref, kbuf_ref, vbuf_ref, m_ref, l_ref, acc_ref, sem_k, sem_v, *, n_pages):
    b = pl.program_id(0)
    spg = kbuf_ref.shape[1]
    n_groups = n_pages // spg
    G = N_HEADS_A // N_KV_A
    m_ref[...] = jnp.full(m_ref.shape, NEG, F32)
    l_ref[...] = jnp.zeros(l_ref.shape, F32)
    acc_ref[...] = jnp.zeros(acc_ref.shape, F32)

    def flash(g, lg, vals):
        m_old = m_ref[g]
        m_new = jnp.maximum(m_old, jnp.max(lg, axis=1, keepdims=True))
        alpha = jnp.exp(m_old - m_new)
        p = jnp.exp(lg - m_new)
        l_ref[g] = alpha * l_ref[g] + jnp.sum(p, axis=1, keepdims=True)
        m_ref[g] = m_new
        acc_ref[g] = alpha * acc_ref[g] + _dot_bf(p, vals)

    def const_bias(g):
        rows = lax.broadcasted_iota(I32, (G * DEC_T, LANE), 0)
        out = jnp.full((G * DEC_T, LANE), c31_ref[g * G], F32)
        for r in range(1, G):
            out = jnp.where(rows >= r * DEC_T, c31_ref[g * G + r], out)
        return out

    _page_copies(ck_hbm, kbuf_ref, sem_k, pt_ref, b, 0, 0, False)
    _page_copies(cv_hbm, vbuf_ref, sem_v, pt_ref, b, 0, 0, False)

    def group(gi, carry):
        slot = gi % 2

        @pl.when(gi + 1 < n_groups)
        def _():
            _page_copies(ck_hbm, kbuf_ref, sem_k, pt_ref, b, gi + 1, 1 - slot, False)
            _page_copies(cv_hbm, vbuf_ref, sem_v, pt_ref, b, gi + 1, 1 - slot, False)

        _page_copies(ck_hbm, kbuf_ref, sem_k, pt_ref, b, gi, slot, True)
        _page_copies(cv_hbm, vbuf_ref, sem_v, pt_ref, b, gi, slot, True)
        mb = jnp.concatenate([mask_ref[0, gi * spg + p] for p in range(spg)], axis=1)
        mb = jnp.concatenate([mb] * G, axis=0)
        for g in range(N_KV_A):
            sl = slice(g * HEAD_DIM, (g + 1) * HEAD_DIM)
            cb = const_bias(g)
            last_b = jnp.where(gi == n_groups - 1, blast_ref[g], cb)
            head = pl.ds(g, PAGE_SIZE, stride=N_KV_A)
            parts = [_dot_nt(q_ref[0, g], kbuf_ref[slot, p, head, :]) + (cb if p + 1 < spg else last_b)
                     for p in range(spg)]
            vals = jnp.concatenate([vbuf_ref[slot, p, head, :] for p in range(spg)], axis=0)
            flash(g, jnp.concatenate(parts, axis=1) + mb, vals)
        return carry

    lax.fori_loop(0, n_groups, group, 0)
    mb_new = jnp.concatenate([mask_ref[0, n_pages]] * G, axis=0)
    for g in range(N_KV_A):
        sl = slice(g * HEAD_DIM, (g + 1) * HEAD_DIM)
        flash(g, _dot_nt(q_ref[0, g], knew_ref[0, :, sl]) + bnew_ref[g] + mb_new, vnew_ref[0, :, sl])
    for g in range(N_KV_A):
        o_ref[0, g] = acc_ref[g] / l_ref[g]


def _sample_attention(page_table, c31, q, k_new, v_new, mask, b_last, b_new, cache_k, cache_v):
    DB, n_pages = page_table.shape
    G = N_HEADS_A // N_KV_A
    full = lambda a: pl.BlockSpec(a.shape, lambda b, pt, c: (0,) * a.ndim)
    grid_spec = pltpu.PrefetchScalarGridSpec(
        num_scalar_prefetch=2,
        grid=(DB,),
        in_specs=[pl.BlockSpec((1, N_KV_A, G * DEC_T, HEAD_DIM), lambda b, pt, c: (b, 0, 0, 0)),
                  pl.BlockSpec((1, LANE, KV_A), lambda b, pt, c: (b, 0, 0)),
                  pl.BlockSpec((1, LANE, KV_A), lambda b, pt, c: (b, 0, 0)),
                  pl.BlockSpec((1, n_pages + 1, DEC_T, LANE), lambda b, pt, c: (b, 0, 0, 0)),
                  full(b_last), full(b_new),
                  pl.BlockSpec(memory_space=pl.ANY), pl.BlockSpec(memory_space=pl.ANY)],
        out_specs=pl.BlockSpec((1, N_KV_A, G * DEC_T, HEAD_DIM), lambda b, pt, c: (b, 0, 0, 0)),
        scratch_shapes=[pltpu.VMEM((2, _pick(n_pages, S_PG, 1), PAGE_SIZE * N_KV_A, HEAD_DIM), F32),
                        pltpu.VMEM((2, _pick(n_pages, S_PG, 1), PAGE_SIZE * N_KV_A, HEAD_DIM), F32),
                        pltpu.VMEM((N_KV_A, G * DEC_T, 1), F32),
                        pltpu.VMEM((N_KV_A, G * DEC_T, 1), F32),
                        pltpu.VMEM((N_KV_A, G * DEC_T, HEAD_DIM), F32),
                        pltpu.SemaphoreType.DMA((2,)),
                        pltpu.SemaphoreType.DMA((2,))],
    )
    return pl.pallas_call(
        functools.partial(_sattn_body, n_pages=n_pages),
        grid_spec=grid_spec,
        out_shape=jax.ShapeDtypeStruct((DB, N_KV_A, G * DEC_T, HEAD_DIM), F32),
        compiler_params=_cparams(("arbitrary",)),
        name="sample_attention",
    )(page_table, c31, q, k_new, v_new, mask, b_last, b_new, cache_k, cache_v)


def _finish_layer(x, o_a, o_b, w_out, ln1_g, ln1_b, w_router, b_router, w1, b1, w2, b2, ln2_g, ln2_b):
    T = x.shape[0]
    E = w1.shape[0]
    half = o_a.shape[1]
    wa = w_out[:half].astype(BF16)
    wb = w_out[half:].astype(BF16)
    wr = jnp.zeros((D_MODEL, LANE), F32).at[:, :E].set(w_router.astype(F32))
    br = jnp.full((1, LANE), NEG, F32).at[0, :E].set(b_router.astype(F32))
    row2 = lambda a: a.astype(F32).reshape(1, -1)
    h, route, counts = _oproj_router(o_a, o_b, x, wa, wb, row2(ln1_g), row2(ln1_b), wr, br,
                                     tm=_pick(T, 256, SUBLANE))
    tm = MOE_TM
    top_e = route[:, R_EXP:R_EXP + TOP_K].astype(I32)
    rank = route[:, R_RANK:R_RANK + TOP_K].astype(I32)
    cnt = counts[0, :E].astype(I32)
    padded = (cnt + tm - 1) // tm * tm
    pend = jnp.cumsum(padded)
    pstart = pend - padded
    dest = (pstart[top_e] + rank).reshape(-1)
    n_blocks = (T * TOP_K + E * (tm - 1) + tm - 1) // tm + 1
    tok = jnp.arange(T * TOP_K, dtype=I32) // TOP_K
    row_tok = jnp.zeros((n_blocks * tm,), I32).at[dest].set(tok)
    blk_e = jnp.minimum(jnp.searchsorted(pend, jnp.arange(n_blocks, dtype=I32) * tm, side='right'),
                        E - 1).astype(I32)
    n_used = (pend[-1:] // tm).astype(I32)
    y_rows = _moe_ffn(blk_e, n_used, row_tok, h, w1.astype(BF16), b1.astype(F32).reshape(E, 1, -1),
                      w2.astype(BF16), b2.astype(F32).reshape(E, 1, -1), tm, MOE_TF)
    return _moe_combine(dest, y_rows, h, route, row2(ln2_g), row2(ln2_b), _pick(T, CMB_TN, SUBLANE))


def _pick(n, cap, mult):
    best = None
    for d in range(mult, min(n, cap) + 1, mult):
        if n % d == 0:
            best = d
    assert best is not None, (n, cap, mult)
    return best


def kernel(x_prompt, x_sample, cache_k, cache_v, cache_kidx, state_conv, state_ssm, page_table, w_in, rel_table,
           w_out, conv_w, A_log, dt_bias, gdn_norm_w, ln1_g, ln1_b, w_router, b_router, w1, b1, w2, b2,
           ln2_g, ln2_b):
    BP, T, D = x_prompt.shape
    DB, TS, _ = x_sample.shape
    n_pool = cache_k.shape[0]
    n_pages = page_table.shape[1]
    past = n_pages * PAGE_SIZE
    assert BP == 1 and D == D_MODEL and TS == DEC_T and T % ATT_TK == 0
    G = N_HEADS_A // N_KV_A
    nq = T // LANE
    att_scale = HEAD_DIM ** -0.5
    idx_scale = IDX_DIM ** -0.5 * N_IDX_HEADS ** -0.5

    x_all = jnp.concatenate([x_prompt.reshape(T, D), x_sample.reshape(DB * TS, D)], axis=0).astype(F32)
    TA = x_all.shape[0]
    proj = _proj(x_all, _arrange_w_in(w_in), _pick(TA, 1280, SUBLANE), 768)
    bias, bias_log2 = _bias_tiles(rel_table)
    c31 = rel_table[N_BUCKETS - 1].astype(F32)

    pp = proj[:T]
    k_p = pp[:, C_KA:C_KA + KV_A]
    v_p = pp[:, C_VA:C_VA + KV_A]
    ki_p = pp[:, C_MISC + M_KI:C_MISC + M_KI + IDX_DIM]
    mask_p = _prompt_indexer(proj, ki_p.astype(BF16), min(TOPK_MAX, T // 4))
    oa_p = _prompt_attention(proj, k_p.astype(BF16), v_p.T.astype(BF16), mask_p, bias_log2)

    ps = proj[T:].reshape(DB, TS, N_PROJ)
    pad_rows = lambda a: jnp.pad(a, ((0, 0), (0, LANE - TS), (0, 0)))
    q_s = (ps[..., C_QA:C_QA + Q_A].reshape(DB, TS, N_KV_A, G, HEAD_DIM) * att_scale)
    q_s = q_s.transpose(0, 2, 3, 1, 4).reshape(DB, N_KV_A, G * TS, HEAD_DIM).astype(BF16)
    k_s = ps[..., C_KA:C_KA + KV_A]
    v_s = ps[..., C_VA:C_VA + KV_A]
    ki_s = ps[..., C_MISC + M_KI:C_MISC + M_KI + IDX_DIM]
    qi_s = ps[..., C_QI:C_QI + QI_W].reshape(DB, TS, N_IDX_HEADS, IDX_DIM)
    qi_s = qi_s.transpose(0, 2, 1, 3).reshape(DB, N_IDX_HEADS * TS, IDX_DIM).astype(BF16)
    w_s = (ps[..., C_MISC + M_WI:C_MISC + M_WI + N_IDX_HEADS] * idx_scale).transpose(0, 2, 1)
    w_s = w_s.reshape(DB, N_IDX_HEADS * TS, 1)
    mask_s = _sample_indexer(page_table, qi_s, w_s, pad_rows(ki_s), cache_kidx.astype(F32),
                             min(TOPK_MAX, (past + TS) // 4))
    to_rows = lambda t: t[:, :, :TS].transpose(0, 2, 1).reshape(N_KV_A, G * TS, LANE)
    o_s = _sample_attention(page_table, c31, q_s, pad_rows(k_s), pad_rows(v_s), mask_s,
                            to_rows(bias[1]), to_rows(bias[0]),
                            cache_k.astype(F32).reshape(n_pool, PAGE_SIZE * N_KV_A, HEAD_DIM),
                            cache_v.astype(F32).reshape(n_pool, PAGE_SIZE * N_KV_A, HEAD_DIM))
    oa_s = o_s.reshape(DB, N_KV_A, G, TS, HEAD_DIM).transpose(0, 3, 1, 2, 4).reshape(DB * TS, Q_A)

    lanes8 = lambda a: jnp.zeros((1, LANE), F32).at[0, M_A:M_A + N_HEADS_B].set(a.astype(F32))
    gdn_args = (conv_w.astype(F32), lanes8(A_log), lanes8(dt_bias), gdn_norm_w.astype(F32).reshape(1, HEAD_DIM))
    ob_p, cb_p, s_p = _gdn(proj, 0, BP, T, jnp.zeros((BP, SUBLANE, CONV_DIM), F32),
                           jnp.zeros((BP, N_HEADS_B, HEAD_DIM, HEAD_DIM), F32), *gdn_args)
    conv0_s = jnp.pad(state_conv.astype(F32), ((0, 0), (SUBLANE - (CONV_W - 1), 0), (0, 0)))
    ob_s, cb_s, s_s = _gdn(proj, T, DB, TS, conv0_s, state_ssm.astype(F32), *gdn_args)

    y_all = _finish_layer(x_all, jnp.concatenate([oa_p, oa_s], axis=0), jnp.concatenate([ob_p, ob_s], axis=0),
                          w_out, ln1_g, ln1_b, w_router, b_router, w1, b1, w2, b2, ln2_g, ln2_b)
    tail = SUBLANE - (CONV_W - 1)
    return (y_all[:T].reshape(BP, T, D), y_all[T:].reshape(DB, TS, D),
            k_p.reshape(BP, T, N_KV_A, HEAD_DIM), v_p.reshape(BP, T, N_KV_A, HEAD_DIM),
            ki_p.reshape(BP, T, IDX_DIM), cb_p[:, tail:], s_p,
            k_s.reshape(DB, TS, N_KV_A, HEAD_DIM), v_s.reshape(DB, TS, N_KV_A, HEAD_DIM), ki_s,
            cb_s[:, tail:], s_s)
```

```python
import functools
import math

import jax
import jax.numpy as jnp
import numpy as np
from jax import lax
from jax.experimental import pallas as pl
from jax.experimental.pallas import tpu as pltpu

F32 = jnp.float32
BF16 = jnp.bfloat16
I32 = jnp.int32
I16 = jnp.int16

D_MODEL = 2048
HEAD_DIM = 128
N_HEADS_A = 8
N_KV_A = 4
N_IDX_HEADS = 16
IDX_DIM = 64
TOPK_MAX = 256
N_BUCKETS = 32
MAX_DISTANCE = 128
N_HEADS_B = 8
CONV_W = 4
CONV_DIM = N_HEADS_B * 3 * HEAD_DIM
TOP_K = 4
D_FF = D_MODEL
SWIGLU_LIMIT = 7.0
SWIGLU_ALPHA = 1.702
PAGE_SIZE = 128
DEPTH = 1
DEEPNORM_ALPHA = (2 * DEPTH) ** 0.25
LN_EPS = 1e-5

Q_A = N_HEADS_A * HEAD_DIM
KV_A = N_KV_A * HEAD_DIM
QI_W = N_IDX_HEADS * IDX_DIM
Z_B = N_HEADS_B * HEAD_DIM

LANE = 128
SUBLANE = 8
PACK16 = 16
VMEM_LIMIT = 56 * 1024 * 1024

C_QA = 0
C_KA = C_QA + Q_A
C_VA = C_KA + KV_A
C_QI = C_VA + KV_A
C_QKV = C_QI + QI_W
C_Z = C_QKV + CONV_DIM
C_MISC = C_Z + Z_B
M_KI, M_WI, M_A, M_B = 0, 64, 80, 88
N_PROJ = 7680

NEG = -1e30
LOG2E = 1.4426950408889634
GDN_C = 128
GDN_HG = 4


def _cparams(sem, vmem=VMEM_LIMIT):
    return pltpu.CompilerParams(dimension_semantics=sem, vmem_limit_bytes=vmem)


def _proj_body(x_ref, w_ref, o_ref, xb_ref):
    @pl.when(pl.program_id(1) == 0)
    def _():
        xb_ref[...] = x_ref[...].astype(BF16)

    o_ref[...] = jnp.dot(xb_ref[...], w_ref[...], preferred_element_type=F32)


def _proj(x, w, tm, tn):
    T, D = x.shape
    N = w.shape[1]
    return pl.pallas_call(
        _proj_body,
        grid=(T // tm, N // tn),
        in_specs=[pl.BlockSpec((tm, D), lambda i, j: (i, 0)),
                  pl.BlockSpec((D, tn), lambda i, j: (0, j))],
        out_specs=pl.BlockSpec((tm, tn), lambda i, j: (i, j)),
        out_shape=jax.ShapeDtypeStruct((T, N), F32),
        scratch_shapes=[pltpu.VMEM((tm, D), BF16)],
        compiler_params=_cparams(("arbitrary", "arbitrary")),
        name="in_proj",
    )(x, w)


def _arrange_w_in(w_in):
    offs = np.cumsum([0, Q_A, KV_A, KV_A, QI_W, IDX_DIM, N_IDX_HEADS, CONV_DIM, Z_B, N_HEADS_B, N_HEADS_B])
    seg = lambda k: w_in[:, offs[k]:offs[k + 1]]
    qa, ka, va, qi, ki, wi, qkv, z, a, b = (seg(k) for k in range(10))
    pad_misc = jnp.zeros((w_in.shape[0], LANE - (IDX_DIM + N_IDX_HEADS + 2 * N_HEADS_B)), w_in.dtype)
    pad_tail = jnp.zeros((w_in.shape[0], N_PROJ - (C_MISC + LANE)), w_in.dtype)
    return jnp.concatenate([qa, ka, va, qi, qkv, z, ki, wi, a, b, pad_misc, pad_tail], axis=1).astype(BF16)


def _bias_body(rel_ref, o_ref, o2_ref):
    d = pl.program_id(0)
    s = lax.broadcasted_iota(I32, (LANE, LANE), 0)
    t = lax.broadcasted_iota(I32, (LANE, LANE), 1)
    n = jnp.maximum(d * LANE + t - s, 0)
    max_exact = N_BUCKETS // 2
    nf = jnp.maximum(n, 1).astype(F32)
    large = max_exact + (jnp.log(nf / max_exact) / math.log(MAX_DISTANCE / max_exact)
                         * (N_BUCKETS - max_exact)).astype(I32)
    large = jnp.minimum(large, N_BUCKETS - 1)
    bucket = jnp.where(n < max_exact, n, large)
    for h in range(N_HEADS_A):
        acc = jnp.zeros((LANE, LANE), F32)
        for b in range(N_BUCKETS):
            acc = jnp.where(bucket == b, rel_ref[b, h], acc)
        o_ref[0, h] = acc
        o2_ref[0, h] = acc * LOG2E


def _bias_tiles(rel_table):
    spec = pl.BlockSpec((1, N_HEADS_A, LANE, LANE), lambda d: (d, 0, 0, 0))
    shape = jax.ShapeDtypeStruct((3, N_HEADS_A, LANE, LANE), F32)
    return pl.pallas_call(
        _bias_body,
        grid=(3,),
        in_specs=[pl.BlockSpec(memory_space=pltpu.SMEM)],
        out_specs=[spec, spec],
        out_shape=[shape, shape],
        compiler_params=_cparams(("arbitrary",)),
        name="rel_bias_tiles",
    )(rel_table.astype(F32))


INT_MIN = -2 ** 31
NEG_INF_KEY = int(np.int32(np.array(-np.inf, np.float32).view(np.int32)) ^ np.int32(0x7FFFFFFF))


def _sortable(x):
    b = pltpu.bitcast(x, I32)
    return b ^ (lax.shift_right_arithmetic(b, 31) & jnp.int32(0x7FFFFFFF))


def _kth_largest_key(count_ge, shape, k):
    def bit_body(b, c):
        cand = c + lax.shift_left(jnp.int32(1), 31 - b)
        return jnp.where(count_ge(cand) >= k, cand, c)

    return lax.fori_loop(0, 32, bit_body, jnp.full(shape, INT_MIN, I32))


IDX_CH = 512


def _pidx_body(qi_ref, misc_ref, ki_ref, o_ref, keys_ref, half_ref, qt_ref, w_ref, *, topk, n_keys):
    i = pl.program_id(0)
    nch = (i * LANE + LANE + IDX_CH - 1) // IDX_CH
    t_idx = i * LANE + lax.broadcasted_iota(I32, (IDX_CH, LANE), 1)
    s_loc = lax.broadcasted_iota(I32, (IDX_CH, LANE), 0)

    for p in range(N_IDX_HEADS // 2):
        xt = qi_ref[:, p * LANE:(p + 1) * LANE].T
        qt_ref[p] = jnp.concatenate([xt[:IDX_DIM], xt[IDX_DIM:]], axis=1).astype(BF16)
    w_ref[...] = misc_ref[...].T * (IDX_DIM ** -0.5 * N_IDX_HEADS ** -0.5)

    def score_chunk(c, carry):
        k0 = pl.multiple_of(c * IDX_CH, IDX_CH)
        kc = ki_ref[pl.ds(k0, IDX_CH), :]
        acc = jnp.zeros((IDX_CH, LANE), F32)
        for p in range(N_IDX_HEADS // 2):
            r = jnp.dot(kc, qt_ref[p], preferred_element_type=F32)
            acc = acc + w_ref[M_WI + 2 * p:M_WI + 2 * p + 1, :] * jnp.maximum(r[:, :LANE], 0.0)
            acc = acc + w_ref[M_WI + 2 * p + 1:M_WI + 2 * p + 2, :] * jnp.maximum(r[:, LANE:], 0.0)
        acc = jnp.where(k0 + s_loc <= t_idx, acc, -jnp.inf)
        keys_ref[pl.ds(k0, IDX_CH), :] = _sortable(acc)
        return carry

    lax.fori_loop(0, nch, score_chunk, 0)

    def count(pred):
        def body(c, cnt):
            k0 = pl.multiple_of(c * IDX_CH, IDX_CH)
            m = pred(keys_ref[pl.ds(k0, IDX_CH), :], k0).astype(I32)
            return cnt + jnp.sum(m.reshape(IDX_CH // SUBLANE, SUBLANE, LANE), axis=0)

        cnt = lax.fori_loop(0, nch, body, jnp.zeros((SUBLANE, LANE), I32))
        return jnp.sum(cnt, axis=0, keepdims=True)

    HALF_MIN = -2 ** 15

    def half_search():
        def count16(cand):
            cand = cand.astype(I16)

            def body(c, cnt):
                k0 = pl.multiple_of(c * IDX_CH, IDX_CH)
                m = jnp.where(half_ref[pl.ds(k0, IDX_CH), :] >= cand, jnp.int16(1), jnp.int16(0))
                parts = [m[r:r + PACK16] for r in range(0, IDX_CH, PACK16)]
                while len(parts) > 1:
                    parts = [a + b for a, b in zip(parts[0::2], parts[1::2])]
                return cnt + parts[0]

            cnt = lax.fori_loop(0, nch, body, jnp.zeros((PACK16, LANE), I16))
            return jnp.sum(cnt.astype(I32), axis=0, keepdims=True)

        def bit_body(b, c):
            cand = c + lax.shift_left(jnp.int32(1), 15 - b)
            return jnp.where(count16(cand) >= topk, cand, c)

        return lax.fori_loop(0, 16, bit_body, jnp.full((1, LANE), HALF_MIN, I32))

    def fill_high(c, carry):
        k0 = pl.multiple_of(c * IDX_CH, IDX_CH)
        half_ref[pl.ds(k0, IDX_CH), :] = lax.shift_right_arithmetic(keys_ref[pl.ds(k0, IDX_CH), :], 16).astype(I16)
        return carry

    lax.fori_loop(0, nch, fill_high, 0)
    hi = half_search()

    def fill_low(c, carry):
        k0 = pl.multiple_of(c * IDX_CH, IDX_CH)
        kb = keys_ref[pl.ds(k0, IDX_CH), :]
        kh = lax.shift_right_arithmetic(kb, 16)
        lo = (kb & 0xFFFF) + HALF_MIN
        lo = jnp.where(kh > hi, -HALF_MIN - 1, jnp.where(kh < hi, HALF_MIN, lo))
        half_ref[pl.ds(k0, IDX_CH), :] = lo.astype(I16)
        return carry

    lax.fori_loop(0, nch, fill_low, 0)
    thr = lax.shift_left(hi, 16) | ((half_search() - HALF_MIN) & 0xFFFF)
    n_ge = count(lambda kb, k0: kb >= thr)
    n_gt = count(lambda kb, k0: kb > thr)
    need = jnp.where(thr == NEG_INF_KEY, 0, topk - n_gt)
    excess = jnp.max(n_ge - n_gt - need) > 0

    def tie_limit():
        nbits = max(1, int(math.ceil(math.log2(n_keys))))
        def bit_body(b, p):
            cand = p + lax.shift_left(jnp.int32(1), nbits - 1 - b)
            f = count(lambda kb, k0: (kb == thr) & (k0 + s_loc < cand))
            return jnp.where(f < need, cand, p)
        return lax.fori_loop(0, nbits, bit_body, jnp.zeros((1, LANE), I32))

    lim = lax.cond(excess, tie_limit, lambda: jnp.full((1, LANE), n_keys, I32))
    lim = jnp.where(need > 0, lim, -1)

    def write_chunk(c, carry):
        k0 = pl.multiple_of(c * IDX_CH, IDX_CH)
        kb = keys_ref[pl.ds(k0, IDX_CH), :]
        sel = (kb > thr) | ((kb == thr) & (k0 + s_loc <= lim))
        o_ref[pl.ds(k0, IDX_CH), :] = jnp.where(sel, 0.0, NEG).astype(o_ref.dtype)
        return carry

    lax.fori_loop(0, nch, write_chunk, 0)

    def fill_chunk(c, carry):
        k0 = pl.multiple_of(c * IDX_CH, IDX_CH)
        o_ref[pl.ds(k0, IDX_CH), :] = jnp.full((IDX_CH, LANE), NEG, o_ref.dtype)
        return carry

    lax.fori_loop(nch, n_keys // IDX_CH, fill_chunk, 0)


def _prompt_indexer(proj, ki, topk):
    T = ki.shape[0]
    return pl.pallas_call(
        functools.partial(_pidx_body, topk=topk, n_keys=T),
        grid=(T // LANE,),
        in_specs=[pl.BlockSpec((LANE, QI_W), lambda i: (i, C_QI // QI_W)),
                  pl.BlockSpec((LANE, LANE), lambda i: (i, C_MISC // LANE)),
                  pl.BlockSpec((T, IDX_DIM), lambda i: (0, 0))],
        out_specs=pl.BlockSpec((T, LANE), lambda i: (0, i)),
        out_shape=jax.ShapeDtypeStruct((T, T), BF16),
        scratch_shapes=[pltpu.VMEM((T, LANE), I32),
                        pltpu.VMEM((T, LANE), I16),
                        pltpu.VMEM((N_IDX_HEADS // 2, IDX_DIM, 2 * LANE), BF16),
                        pltpu.VMEM((LANE, LANE), F32)],
        compiler_params=_cparams(("arbitrary",)),
        name="prompt_indexer",
    )(proj, proj, ki)


ATT_TK = 512


def _pattn_body(ii_ref, jj_ref, q_ref, k_ref, vt_ref, mask_ref, bias_ref, o_ref, qt_ref, m_ref, l_ref, acc_ref,
                lg_ref, pb_ref):
    s = pl.program_id(0)
    i = ii_ref[s]
    j = jj_ref[s]
    G = N_HEADS_A // N_KV_A
    NU = ATT_TK // LANE

    @pl.when(j == 0)
    def _():
        m_ref[...] = jnp.full(m_ref.shape, NEG, F32)
        l_ref[...] = jnp.zeros(l_ref.shape, F32)
        acc_ref[...] = jnp.zeros(acc_ref.shape, F32)
        for h in range(N_HEADS_A):
            qh = q_ref[:, h * HEAD_DIM:(h + 1) * HEAD_DIM] * (HEAD_DIM ** -0.5 * LOG2E)
            qt_ref[h // G, :, (h % G) * LANE:(h % G + 1) * LANE] = qh.T.astype(BF16)

    bidx = [jnp.clip((i * LANE - j * ATT_TK - u * LANE) // LANE, 0, 2) for u in range(NU)]
    m_olds = [m_ref[g] for g in range(N_KV_A)]
    mxs = list(m_olds)
    for u in range(NU):
        rows = slice(u * LANE, (u + 1) * LANE)
        mbu = mask_ref[rows, :].astype(F32)
        for g in range(N_KV_A):
            st = jnp.dot(k_ref[rows, g * HEAD_DIM:(g + 1) * HEAD_DIM], qt_ref[g], preferred_element_type=F32)
            lgu = jnp.concatenate([st[:, r * LANE:(r + 1) * LANE] + (bias_ref[bidx[u], g * G + r] + mbu)
                                   for r in range(G)], axis=1)
            lg_ref[g, rows, :] = lgu
            mxs[g] = jnp.maximum(mxs[g], jnp.max(lgu, axis=0, keepdims=True))
    alphas = [jnp.exp2(m_olds[g] - mxs[g]) for g in range(N_KV_A)]
    lsums = [jnp.zeros_like(mxs[g]) for g in range(N_KV_A)]
    zero = lax.shift_right_arithmetic(j, 31) * LANE
    for u in range(NU):
        rows = slice(u * LANE, (u + 1) * LANE)
        for g in range(N_KV_A):
            p = jnp.exp2(lg_ref[g, pl.ds(pl.multiple_of(zero + u * LANE, LANE), LANE), :] - mxs[g])
            lsums[g] = lsums[g] + jnp.sum(p, axis=0, keepdims=True)
            pb_ref[g, rows, :] = p.astype(BF16)
    for g in range(N_KV_A):
        l_ref[g] = alphas[g] * l_ref[g] + lsums[g]
        m_ref[g] = mxs[g]
        vg = vt_ref[g * HEAD_DIM:(g + 1) * HEAD_DIM, :]
        acc_ref[g] = alphas[g] * acc_ref[g] + jnp.dot(vg, pb_ref[g], preferred_element_type=F32)

    @pl.when((j + 1) * ATT_TK > i * LANE + LANE - 1)
    def _():
        for h in range(N_HEADS_A):
            cs = slice((h % G) * LANE, (h % G + 1) * LANE)
            o_ref[:, h * HEAD_DIM:(h + 1) * HEAD_DIM] = (acc_ref[h // G][:, cs] / l_ref[h // G][:, cs]).T


def _prompt_attention(proj, k, v_t, mask, bias):
    T = k.shape[0]
    nq = T // LANE
    ii, jj = [], []
    for i in range(nq):
        for j in range((i * LANE + LANE - 1) // ATT_TK + 1):
            ii.append(i)
            jj.append(j)
    ii = jnp.asarray(np.array(ii, np.int32))
    jj = jnp.asarray(np.array(jj, np.int32))
    G = N_HEADS_A // N_KV_A
    grid_spec = pltpu.PrefetchScalarGridSpec(
        num_scalar_prefetch=2,
        grid=(int(ii.shape[0]),),
        in_specs=[pl.BlockSpec((LANE, Q_A), lambda s, ii, jj: (ii[s], C_QA // Q_A)),
                  pl.BlockSpec((ATT_TK, KV_A), lambda s, ii, jj: (jj[s], 0)),
                  pl.BlockSpec((KV_A, ATT_TK), lambda s, ii, jj: (0, jj[s])),
                  pl.BlockSpec((ATT_TK, LANE), lambda s, ii, jj: (jj[s], ii[s])),
                  pl.BlockSpec((3, N_HEADS_A, LANE, LANE), lambda s, ii, jj: (0, 0, 0, 0))],
        out_specs=pl.BlockSpec((LANE, Q_A), lambda s, ii, jj: (ii[s], 0)),
        scratch_shapes=[pltpu.VMEM((N_KV_A, HEAD_DIM, G * LANE), BF16),
                        pltpu.VMEM((N_KV_A, 1, G * LANE), F32),
                        pltpu.VMEM((N_KV_A, 1, G * LANE), F32),
                        pltpu.VMEM((N_KV_A, HEAD_DIM, G * LANE), F32),
                        pltpu.VMEM((N_KV_A, ATT_TK, G * LANE), F32),
                        pltpu.VMEM((N_KV_A, ATT_TK, G * LANE), BF16)],
    )
    return pl.pallas_call(
        _pattn_body,
        grid_spec=grid_spec,
        out_shape=jax.ShapeDtypeStruct((T, Q_A), F32),
        compiler_params=_cparams(("arbitrary",)),
        name="prompt_attention",
    )(ii, jj, proj, k, v_t, mask, bias)


def _split3(a):
    a1 = a.astype(BF16)
    r = a - a1.astype(F32)
    a2 = r.astype(BF16)
    a3 = (r - a2.astype(F32)).astype(BF16)
    return a1, a2, a3


def _dot_f32(a, b):
    a1, a2, a3 = _split3(a)
    b1, b2, b3 = _split3(b)
    d = lambda p, q: jnp.dot(p, q, preferred_element_type=F32)
    return (d(a1, b1) + (d(a1, b2) + d(a2, b1))) + ((d(a1, b3) + d(a3, b1)) + d(a2, b2))


def _split2(a):
    a1 = a.astype(BF16)
    return a1, (a - a1.astype(F32)).astype(BF16)


def _dot_3p(a, b):
    a1, a2 = _split2(a)
    b1, b2 = _split2(b)
    d = lambda p, q: jnp.dot(p, q, preferred_element_type=F32)
    return d(a1, b1) + (d(a1, b2) + d(a2, b1))


def _dot_bf(a, b):
    return jnp.dot(a.astype(BF16), b.astype(BF16), preferred_element_type=F32)


def _dot_nt(a, b):
    return lax.dot_general(a.astype(BF16), b.astype(BF16), (((1,), (1,)), ((), ())), preferred_element_type=F32)


def _sigmoid(x):
    return 1.0 / (1.0 + jnp.exp(-x))


def _softplus(x):
    return jnp.maximum(x, 0.0) + jnp.log(1.0 + jnp.exp(-jnp.abs(x)))


def _gdn_body(qkv_ref, z_ref, misc_ref, conv0_ref, s0_ref, cw_ref, arow_ref, dtrow_ref, nw_ref,
              o_ref, convo_ref, so_ref, xbuf_ref, s_ref, *, n_valid):
    C = GDN_C
    n = pl.program_id(1)
    H = N_HEADS_B

    @pl.when(n == 0)
    def _():
        xbuf_ref[0:SUBLANE, :] = conv0_ref[0]
        s_ref[...] = s0_ref[0]
        if n_valid < C:
            xbuf_ref[SUBLANE:, :] = jnp.zeros((C, CONV_DIM), F32)

    xbuf_ref[SUBLANE:SUBLANE + n_valid, :] = qkv_ref[...]
    conv = xbuf_ref[5:5 + C, :] * cw_ref[0:1, :]
    for i in range(1, CONV_W):
        conv = conv + xbuf_ref[5 + i:5 + i + C, :] * cw_ref[i:i + 1, :]
    tail = xbuf_ref[n_valid:n_valid + SUBLANE, :]
    xbuf_ref[0:SUBLANE, :] = tail
    convo_ref[0] = tail

    row = lax.broadcasted_iota(I32, (C, LANE), 0)
    col = lax.broadcasted_iota(I32, (C, LANE), 1)
    live = row < n_valid
    hact = conv * _sigmoid(conv)

    misc = misc_ref[...]
    if n_valid < C:
        misc = jnp.concatenate([misc, jnp.zeros((C - n_valid, LANE), F32)], axis=0)
    g_full = jnp.where(live, -jnp.exp(arow_ref[...]) * _softplus(misc + dtrow_ref[...]), 0.0)
    beta_full = jnp.where(live, _sigmoid(misc), 0.0)
    tril = (row >= col).astype(F32)
    gc_full = _dot_f32(tril, g_full)
    gc_t = gc_full.T
    strict = row > col
    incl = row >= col

    def setup(hd):
        sl = slice(hd * HEAD_DIM, (hd + 1) * HEAD_DIM)
        q = hact[:, sl]
        k = hact[:, H * HEAD_DIM + hd * HEAD_DIM:H * HEAD_DIM + (hd + 1) * HEAD_DIM]
        v = hact[:, 2 * H * HEAD_DIM + hd * HEAD_DIM:2 * H * HEAD_DIM + (hd + 1) * HEAD_DIM]
        q = q * lax.rsqrt(jnp.sum(q * q, -1, keepdims=True) + 1e-6) * HEAD_DIM ** -0.5
        k = k * lax.rsqrt(jnp.sum(k * k, -1, keepdims=True) + 1e-6)
        q = jnp.where(live, q, 0.0)
        k = jnp.where(live, k, 0.0)
        gc_col = jnp.broadcast_to(gc_full[:, M_A + hd:M_A + hd + 1], (C, LANE))
        gc_row = jnp.broadcast_to(gc_t[M_A + hd:M_A + hd + 1, :], (C, LANE))
        gc_last = jnp.broadcast_to(gc_full[C - 1:C, M_A + hd:M_A + hd + 1], (C, LANE))
        beta = jnp.broadcast_to(beta_full[:, M_B + hd:M_B + hd + 1], (C, LANE))
        decay = jnp.where(incl, jnp.exp(jnp.where(incl, gc_col - gc_row, 0.0)), 0.0)
        kk = _dot_nt(k, k)
        qk = _dot_nt(q, k) * decay
        e_gc = jnp.exp(gc_col)
        m = jnp.where(strict, -(beta * decay * kk), 0.0)
        y = jnp.concatenate([v * beta, k * (beta * e_gc)], axis=1)
        return dict(q=q, k=k, qk=qk, e_gc=e_gc, gc_col=gc_col, gc_last=gc_last, m=m, y=y)

    def finish(hd, st):
        sl = slice(hd * HEAD_DIM, (hd + 1) * HEAD_DIM)
        u0 = st["y"][:, :HEAD_DIM]
        w = st["y"][:, HEAD_DIM:]
        s_old = s_ref[hd]
        u = u0 - _dot_bf(w, s_old)
        o = _dot_bf(st["q"] * st["e_gc"], s_old) + _dot_bf(st["qk"], u)
        kd = st["k"] * jnp.exp(st["gc_last"] - st["gc_col"])
        s_ref[hd] = jnp.exp(st["gc_last"]) * s_old + _dot_bf(kd.T, u)
        o = o * lax.rsqrt(jnp.mean(o * o, -1, keepdims=True) + 1e-6) * nw_ref[...]
        zz = z_ref[:, sl]
        o_ref[:, sl] = o[:n_valid] * (zz * _sigmoid(zz))

    nlev = int(math.log2(C))
    for h0 in range(0, H, GDN_HG):
        sts = [setup(hd) for hd in range(h0, h0 + GDN_HG)]
        for lvl in range(nlev):
            for st in sts:
                if lvl + 1 < nlev:
                    mp = _dot_3p(st["m"], jnp.concatenate([st["m"], st["y"]], axis=1))
                    st["y"] = st["y"] + mp[:, C:]
                    st["m"] = mp[:, :C]
                else:
                    st["y"] = st["y"] + _dot_3p(st["m"], st["y"])
        for i, st in enumerate(sts):
            finish(h0 + i, st)

    so_ref[0] = s_ref[...]


def _gdn(proj, row0, B, T, conv0, s0, conv_w, a_row, dt_row, norm_w):
    n_valid = min(T, GDN_C)
    nblk = T // n_valid
    r0 = row0 // n_valid
    rb = lambda b, n: r0 + b * nblk + n
    return pl.pallas_call(
        functools.partial(_gdn_body, n_valid=n_valid),
        grid=(B, nblk),
        in_specs=[pl.BlockSpec((n_valid, CONV_DIM), lambda b, n: (rb(b, n), C_QKV // CONV_DIM)),
                  pl.BlockSpec((n_valid, Z_B), lambda b, n: (rb(b, n), C_Z // Z_B)),
                  pl.BlockSpec((n_valid, LANE), lambda b, n: (rb(b, n), C_MISC // LANE)),
                  pl.BlockSpec((1, SUBLANE, CONV_DIM), lambda b, n: (b, 0, 0)),
                  pl.BlockSpec((1, N_HEADS_B, HEAD_DIM, HEAD_DIM), lambda b, n: (b, 0, 0, 0)),
                  pl.BlockSpec((CONV_W, CONV_DIM), lambda b, n: (0, 0)),
                  pl.BlockSpec((1, LANE), lambda b, n: (0, 0)),
                  pl.BlockSpec((1, LANE), lambda b, n: (0, 0)),
                  pl.BlockSpec((1, HEAD_DIM), lambda b, n: (0, 0))],
        out_specs=[pl.BlockSpec((n_valid, Z_B), lambda b, n: (b * nblk + n, 0)),
                   pl.BlockSpec((1, SUBLANE, CONV_DIM), lambda b, n: (b, 0, 0)),
                   pl.BlockSpec((1, N_HEADS_B, HEAD_DIM, HEAD_DIM), lambda b, n: (b, 0, 0, 0))],
        out_shape=[jax.ShapeDtypeStruct((B * T, Z_B), F32),
                   jax.ShapeDtypeStruct((B, SUBLANE, CONV_DIM), F32),
                   jax.ShapeDtypeStruct((B, N_HEADS_B, HEAD_DIM, HEAD_DIM), F32)],
        scratch_shapes=[pltpu.VMEM((SUBLANE + GDN_C, CONV_DIM), F32),
                        pltpu.VMEM((N_HEADS_B, HEAD_DIM, HEAD_DIM), F32)],
        compiler_params=_cparams(("arbitrary", "arbitrary")),
        name="gdn_mixer",
    )(proj, proj, proj, conv0, s0, conv_w, a_row, dt_row, norm_w)


R_EXP, R_GATE, R_RANK = 0, 4, 8


def _layer_norm(x, g, b):
    mu = jnp.mean(x, -1, keepdims=True)
    xc = x - mu
    var = jnp.mean(xc * xc, -1, keepdims=True)
    return xc * lax.rsqrt(var + LN_EPS) * g + b


def _oproj_body(oa_ref, ob_ref, x_ref, wa_ref, wb_ref, g_ref, b_ref, wr_ref, br_ref,
                h_ref, route_ref, cnt_ref, run_ref):
    i = pl.program_id(0)
    tm = x_ref.shape[0]

    @pl.when(i == 0)
    def _():
        run_ref[...] = jnp.zeros(run_ref.shape, F32)

    mix = _dot_bf(oa_ref[...], wa_ref[...]) + _dot_bf(ob_ref[...], wb_ref[...])
    h = _layer_norm(DEEPNORM_ALPHA * x_ref[...] + mix, g_ref[...], b_ref[...])
    h_ref[...] = h
    logits = _dot_f32(h, wr_ref[...]) + br_ref[...]
    lane = lax.broadcasted_iota(I32, (tm, LANE), 1)
    work = logits
    tops, idxs = [], []
    for _ in range(TOP_K):
        m = jnp.max(work, axis=1, keepdims=True)
        idx = jnp.min(jnp.where(work == m, lane, LANE), axis=1, keepdims=True)
        tops.append(m)
        idxs.append(idx)
        work = jnp.where(lane == idx, -jnp.inf, work)
    es = [jnp.exp(t - tops[0]) for t in tops]
    denom = es[0] + es[1] + es[2] + es[3]
    onehot = jnp.zeros((tm, LANE), F32)
    for idx in idxs:
        onehot = onehot + (lane == idx).astype(F32)
    r = lax.broadcasted_iota(I32, (tm, tm), 0)
    c = lax.broadcasted_iota(I32, (tm, tm), 1)
    before = jnp.dot((r > c).astype(BF16), onehot.astype(BF16), preferred_element_type=F32) + run_ref[...]
    rec = jnp.zeros((tm, LANE), F32)
    for k in range(TOP_K):
        rank = jnp.sum(jnp.where(lane == idxs[k], before, 0.0), axis=1, keepdims=True)
        rec = jnp.where(lane == R_EXP + k, idxs[k].astype(F32), rec)
        rec = jnp.where(lane == R_GATE + k, es[k] / denom, rec)
        rec = jnp.where(lane == R_RANK + k, rank, rec)
    route_ref[...] = rec
    run_ref[...] = run_ref[...] + jnp.sum(onehot, axis=0, keepdims=True)
    cnt_ref[...] = run_ref[...]


def _oproj_router(o_a, o_b, x, wa, wb, g1, b1, wr, br, tm):
    T = x.shape[0]
    row = lambda w: pl.BlockSpec((tm, w), lambda i: (i, 0))
    full = lambda a: pl.BlockSpec(a.shape, lambda i: (0,) * a.ndim)
    return pl.pallas_call(
        _oproj_body,
        grid=(T // tm,),
        in_specs=[row(o_a.shape[1]), row(o_b.shape[1]), row(D_MODEL),
                  full(wa), full(wb), full(g1), full(b1), full(wr), full(br)],
        out_specs=[row(D_MODEL), row(LANE), pl.BlockSpec((1, LANE), lambda i: (0, 0))],
        out_shape=[jax.ShapeDtypeStruct((T, D_MODEL), F32),
                   jax.ShapeDtypeStruct((T, LANE), F32),
                   jax.ShapeDtypeStruct((1, LANE), F32)],
        scratch_shapes=[pltpu.VMEM((1, LANE), F32)],
        compiler_params=_cparams(("arbitrary",)),
        name="out_proj_router",
    )(o_a, o_b, x, wa, wb, g1, b1, wr, br)


MOE_TM = 512
MOE_TF = 512
CMB_TN = 256


def _row_copies(tok_ref, h_hbm, xbuf_ref, sem, blk, slot, wait):
    tm = xbuf_ref.shape[1]
    base = blk * tm

    def body(r, carry):
        t = 0 if wait else tok_ref[base + r]
        cp = pltpu.make_async_copy(h_hbm.at[pl.ds(t, 1)], xbuf_ref.at[slot, pl.ds(r, 1)], sem.at[slot])
        cp.wait() if wait else cp.start()
        return carry

    lax.fori_loop(0, tm, body, 0, unroll=8)


def _ffn_body(be_ref, nu_ref, tok_ref, h_hbm, w1g_ref, w1l_ref, b1g_ref, b1l_ref, w2_ref, b2_ref, y_ref,
              xbuf_ref, xb_ref, sem):
    b = pl.program_id(0)
    f = pl.program_id(1)
    nu = nu_ref[0]
    first = f == 0
    tm = xbuf_ref.shape[1]
    tq = tm // (D_FF // w1g_ref.shape[2])

    @pl.when(first & (b == 0))
    def _():
        _row_copies(tok_ref, h_hbm, xbuf_ref, sem, 0, 0, False)

    @pl.when(first & (b < nu))
    def _():
        _row_copies(tok_ref, h_hbm, xbuf_ref, sem, b, b % 2, True)
        xb_ref[...] = xbuf_ref[b % 2].astype(BF16)

    @pl.when(first & (b == nu))
    def _():
        _row_copies(tok_ref, h_hbm, xbuf_ref, sem, 0, (nu % 2), True)

    @pl.when(b < nu)
    def _():
        for r in range(tq):
            row = f * tq + r
            pltpu.make_async_copy(h_hbm.at[pl.ds(tok_ref[(b + 1) * tm + row], 1)],
                                  xbuf_ref.at[(b + 1) % 2, pl.ds(row, 1)], sem.at[(b + 1) % 2]).start()
        x = xb_ref[...]
        hg = jnp.dot(x, w1g_ref[0].astype(BF16), preferred_element_type=F32) + b1g_ref[0]
        hl = jnp.dot(x, w1l_ref[0].astype(BF16), preferred_element_type=F32) + b1l_ref[0]
        glu = jnp.minimum(hg, SWIGLU_LIMIT)
        lin = jnp.clip(hl, -SWIGLU_LIMIT, SWIGLU_LIMIT)
        act = glu * _sigmoid(SWIGLU_ALPHA * glu) * (lin + 1.0)
        part = jnp.dot(act.astype(BF16), w2_ref[0].astype(BF16), preferred_element_type=F32)

        @pl.when(f == 0)
        def _():
            y_ref[...] = part + b2_ref[0]

        @pl.when(f > 0)
        def _():
            y_ref[...] = y_ref[...] + part

    @pl.when((b >= nu_ref[0]) & (f == 0))
    def _():
        y_ref[...] = jnp.zeros(y_ref.shape, F32)


def _moe_ffn(blk_e, n_used, row_tok, h, w1, b1, w2, b2, tm, tf):
    n_rows = row_tok.shape[0]
    D = h.shape[1]
    nf = D_FF // tf
    bb = lambda b, nu: jnp.minimum(b, nu[0] - 1)
    ff = lambda b, f, nu: jnp.where(b < nu[0], f, nf - 1)
    grid_spec = pltpu.PrefetchScalarGridSpec(
        num_scalar_prefetch=3,
        grid=(n_rows // tm, nf),
        in_specs=[pl.BlockSpec(memory_space=pl.ANY),
                  pl.BlockSpec((1, D, tf), lambda b, f, be, nu, tok: (be[bb(b, nu)], 0, ff(b, f, nu))),
                  pl.BlockSpec((1, D, tf), lambda b, f, be, nu, tok: (be[bb(b, nu)], 0, nf + ff(b, f, nu))),
                  pl.BlockSpec((1, 1, tf), lambda b, f, be, nu, tok: (be[bb(b, nu)], 0, ff(b, f, nu))),
                  pl.BlockSpec((1, 1, tf), lambda b, f, be, nu, tok: (be[bb(b, nu)], 0, nf + ff(b, f, nu))),
                  pl.BlockSpec((1, tf, D), lambda b, f, be, nu, tok: (be[bb(b, nu)], ff(b, f, nu), 0)),
                  pl.BlockSpec((1, 1, D), lambda b, f, be, nu, tok: (be[bb(b, nu)], 0, 0))],
        out_specs=pl.BlockSpec((tm, D), lambda b, f, be, nu, tok: (b, 0)),
        scratch_shapes=[pltpu.VMEM((2, tm, D), F32),
                        pltpu.VMEM((tm, D), BF16),
                        pltpu.SemaphoreType.DMA((2,))],
    )
    return pl.pallas_call(
        _ffn_body,
        grid_spec=grid_spec,
        out_shape=jax.ShapeDtypeStruct((n_rows, D), F32),
        compiler_params=_cparams(("arbitrary", "arbitrary")),
        name="moe_ffn",
    )(blk_e, n_used, row_tok, h, w1, w1, b1, b1, w2, b2)


def _combine_body(dest_ref, y_hbm, h_ref, route_ref, g_ref, b_ref, o_ref, buf_ref, sem):
    i = pl.program_id(0)
    tn = h_ref.shape[0]

    def issue(t, carry):
        for k in range(TOP_K):
            d = dest_ref[(i * tn + t) * TOP_K + k]
            pltpu.make_async_copy(y_hbm.at[pl.ds(d, 1)], buf_ref.at[k, pl.ds(t, 1)], sem).start()
        return carry

    lax.fori_loop(0, tn, issue, 0)

    def drain(t, carry):
        for k in range(TOP_K):
            pltpu.make_async_copy(y_hbm.at[pl.ds(0, 1)], buf_ref.at[k, pl.ds(t, 1)], sem).wait()
        return carry

    lax.fori_loop(0, tn, drain, 0)
    route = route_ref[...]
    f = route[:, R_GATE:R_GATE + 1] * buf_ref[0]
    for k in range(1, TOP_K):
        f = f + route[:, R_GATE + k:R_GATE + k + 1] * buf_ref[k]
    o_ref[...] = _layer_norm(DEEPNORM_ALPHA * h_ref[...] + f, g_ref[...], b_ref[...])


def _moe_combine(dest, y_rows, h, route, g2, b2, tn):
    T, D = h.shape
    grid_spec = pltpu.PrefetchScalarGridSpec(
        num_scalar_prefetch=1,
        grid=(T // tn,),
        in_specs=[pl.BlockSpec(memory_space=pl.ANY),
                  pl.BlockSpec((tn, D), lambda i, d: (i, 0)),
                  pl.BlockSpec((tn, LANE), lambda i, d: (i, 0)),
                  pl.BlockSpec((1, D), lambda i, d: (0, 0)),
                  pl.BlockSpec((1, D), lambda i, d: (0, 0))],
        out_specs=pl.BlockSpec((tn, D), lambda i, d: (i, 0)),
        scratch_shapes=[pltpu.VMEM((TOP_K, tn, D), F32), pltpu.SemaphoreType.DMA(())],
    )
    return pl.pallas_call(
        _combine_body,
        grid_spec=grid_spec,
        out_shape=jax.ShapeDtypeStruct((T, D), F32),
        compiler_params=_cparams(("arbitrary",)),
        name="moe_combine",
    )(dest, y_rows, h, route, g2, b2)


S_PG = 16
DEC_T = 8


def _page_copies(cache_hbm, buf_ref, sem, pt_ref, b, gi, slot, wait):
    spg = buf_ref.shape[1]
    for p in range(spg):
        src = cache_hbm.at[0] if wait else cache_hbm.at[pt_ref[b, gi * spg + p]]
        cp = pltpu.make_async_copy(src, buf_ref.at[slot, p], sem.at[slot])
        cp.wait() if wait else cp.start()


def _sidx_body(pt_ref, q_ref, w_ref, knew_ref, cache_hbm, o_ref, buf_ref, keys_ref, sem, *, topk, n_pages):
    b = pl.program_id(0)
    spg = buf_ref.shape[1]
    n_groups = n_pages // spg
    ncp = keys_ref.shape[0]
    q = q_ref[0]
    wl = jnp.broadcast_to(w_ref[0], (N_IDX_HEADS * DEC_T, LANE))

    def scores(kp):
        r = _dot_nt(q, kp)
        r = wl * jnp.maximum(r, 0.0)
        acc = r[0:DEC_T]
        for h in range(1, N_IDX_HEADS):
            acc = acc + r[h * DEC_T:(h + 1) * DEC_T]
        return acc

    _page_copies(cache_hbm, buf_ref, sem, pt_ref, b, 0, 0, False)

    def group(gi, carry):
        slot = gi % 2

        @pl.when(gi + 1 < n_groups)
        def _():
            _page_copies(cache_hbm, buf_ref, sem, pt_ref, b, gi + 1, 1 - slot, False)

        _page_copies(cache_hbm, buf_ref, sem, pt_ref, b, gi, slot, True)
        for p in range(spg):
            keys_ref[gi * spg + p] = _sortable(scores(buf_ref[slot, p]))
        return carry

    lax.fori_loop(0, n_groups, group, 0)
    row = lax.broadcasted_iota(I32, (DEC_T, LANE), 0)
    lane = lax.broadcasted_iota(I32, (DEC_T, LANE), 1)
    keys_ref[n_pages] = _sortable(jnp.where(lane <= row, scores(knew_ref[0]), -jnp.inf))
    for c in range(n_pages + 1, ncp):
        keys_ref[c] = jnp.full((DEC_T, LANE), INT_MIN, I32)

    U = 8

    def count(pred):
        def body(c, cnt):
            c0 = pl.multiple_of(c * U, U)
            kb = keys_ref[pl.ds(c0, U)]
            pos = (c0 + lax.broadcasted_iota(I32, (U, DEC_T, LANE), 0)) * LANE \
                + lax.broadcasted_iota(I32, (U, DEC_T, LANE), 2)
            return cnt + jnp.sum(pred(kb, pos).astype(I32), axis=0)

        cnt = lax.fori_loop(0, ncp // U, body, jnp.zeros((DEC_T, LANE), I32))
        return jnp.sum(cnt, axis=1, keepdims=True)

    thr = _kth_largest_key(lambda c: count(lambda kb, pos: kb >= c), (DEC_T, 1), topk)
    n_ge = count(lambda kb, pos: kb >= thr)
    n_gt = count(lambda kb, pos: kb > thr)
    need = jnp.where(thr == NEG_INF_KEY, 0, topk - n_gt)
    excess = jnp.max(n_ge - n_gt - need) > 0
    n_keys = (n_pages + 1) * LANE

    def tie_limit():
        nbits = max(1, int(math.ceil(math.log2(n_keys))))
        def bit_body(bi, p):
            cand = p + lax.shift_left(jnp.int32(1), nbits - 1 - bi)
            f = count(lambda kb, pos: (kb == thr) & (pos < cand))
            return jnp.where(f < need, cand, p)
        return lax.fori_loop(0, nbits, bit_body, jnp.zeros((DEC_T, 1), I32))

    lim = lax.cond(excess, tie_limit, lambda: jnp.full((DEC_T, 1), n_keys, I32))
    lim = jnp.where(need > 0, lim, -1)

    def write(c, carry):
        kb = keys_ref[c]
        pos = c * LANE + lane
        sel = (kb > thr) | ((kb == thr) & (pos <= lim))
        o_ref[0, c] = jnp.where(sel, 0.0, NEG)
        return carry

    lax.fori_loop(0, n_pages + 1, write, 0)


def _sample_indexer(page_table, q, w, k_new, cache_kidx, topk):
    DB, n_pages = page_table.shape
    ncp = (n_pages + 1 + 7) // 8 * 8
    grid_spec = pltpu.PrefetchScalarGridSpec(
        num_scalar_prefetch=1,
        grid=(DB,),
        in_specs=[pl.BlockSpec((1, N_IDX_HEADS * DEC_T, IDX_DIM), lambda b, pt: (b, 0, 0)),
                  pl.BlockSpec((1, N_IDX_HEADS * DEC_T, 1), lambda b, pt: (b, 0, 0)),
                  pl.BlockSpec((1, LANE, IDX_DIM), lambda b, pt: (b, 0, 0)),
                  pl.BlockSpec(memory_space=pl.ANY)],
        out_specs=pl.BlockSpec((1, n_pages + 1, DEC_T, LANE), lambda b, pt: (b, 0, 0, 0)),
        scratch_shapes=[pltpu.VMEM((2, _pick(n_pages, S_PG, 1), PAGE_SIZE, IDX_DIM), F32),
                        pltpu.VMEM((ncp, DEC_T, LANE), I32),
                        pltpu.SemaphoreType.DMA((2,))],
    )
    return pl.pallas_call(
        functools.partial(_sidx_body, topk=topk, n_pages=n_pages),
        grid_spec=grid_spec,
        out_shape=jax.ShapeDtypeStruct((DB, n_pages + 1, DEC_T, LANE), F32),
        compiler_params=_cparams(("arbitrary",)),
        name="sample_indexer",
    )(page_table, q, w, k_new, cache_kidx)


def _sattn_body(pt_ref, c31_ref, q_ref, knew_ref, vnew_ref, mask_ref, blast_ref, bnew_ref, ck_hbm, cv_hbm,
                o_ref, kbuf_ref, vbuf_ref, m_ref, l_ref, acc_ref, sem_k, sem_v, *, n_pages):
    b = pl.program_id(0)
    spg = kbuf_ref.shape[1]
    n_groups = n_pages // spg
    G = N_HEADS_A // N_KV_A
    m_ref[...] = jnp.full(m_ref.shape, NEG, F32)
    l_ref[...] = jnp.zeros(l_ref.shape, F32)
    acc_ref[...] = jnp.zeros(acc_ref.shape, F32)

    def flash(g, lg, vals):
        m_old = m_ref[g]
        m_new = jnp.maximum(m_old, jnp.max(lg, axis=1, keepdims=True))
        alpha = jnp.exp(m_old - m_new)
        p = jnp.exp(lg - m_new)
        l_ref[g] = alpha * l_ref[g] + jnp.sum(p, axis=1, keepdims=True)
        m_ref[g] = m_new
        acc_ref[g] = alpha * acc_ref[g] + _dot_bf(p, vals)

    def const_bias(g):
        rows = lax.broadcasted_iota(I32, (G * DEC_T, LANE), 0)
        out = jnp.full((G * DEC_T, LANE), c31_ref[g * G], F32)
        for r in range(1, G):
            out = jnp.where(rows >= r * DEC_T, c31_ref[g * G + r], out)
        return out

    _page_copies(ck_hbm, kbuf_ref, sem_k, pt_ref, b, 0, 0, False)
    _page_copies(cv_hbm, vbuf_ref, sem_v, pt_ref, b, 0, 0, False)

    def group(gi, carry):
        slot = gi % 2

        @pl.when(gi + 1 < n_groups)
        def _():
            _page_copies(ck_hbm, kbuf_ref, sem_k, pt_ref, b, gi + 1, 1 - slot, False)
            _page_copies(cv_hbm, vbuf_ref, sem_v, pt_ref, b, gi + 1, 1 - slot, False)

        _page_copies(ck_hbm, kbuf_ref, sem_k, pt_ref, b, gi, slot, True)
        _page_copies(cv_hbm, vbuf_ref, sem_v, pt_ref, b, gi, slot, True)
        mb = jnp.concatenate([mask_ref[0, gi * spg + p] for p in range(spg)], axis=1)
        mb = jnp.concatenate([mb] * G, axis=0)
        for g in range(N_KV_A):
            sl = slice(g * HEAD_DIM, (g + 1) * HEAD_DIM)
            cb = const_bias(g)
            last_b = jnp.where(gi == n_groups - 1, blast_ref[g], cb)
            head = pl.ds(g, PAGE_SIZE, stride=N_KV_A)
            parts = [_dot_nt(q_ref[0, g], kbuf_ref[slot, p, head, :]) + (cb if p + 1 < spg else last_b)
                     for p in range(spg)]
            vals = jnp.concatenate([vbuf_ref[slot, p, head, :] for p in range(spg)], axis=0)
            flash(g, jnp.concatenate(parts, axis=1) + mb, vals)
        return carry

    lax.fori_loop(0, n_groups, group, 0)
    mb_new = jnp.concatenate([mask_ref[0, n_pages]] * G, axis=0)
    for g in range(N_KV_A):
        sl = slice(g * HEAD_DIM, (g + 1) * HEAD_DIM)
        flash(g, _dot_nt(q_ref[0, g], knew_ref[0, :, sl]) + bnew_ref[g] + mb_new, vnew_ref[0, :, sl])
    for g in range(N_KV_A):
        o_ref[0, g] = acc_ref[g] / l_ref[g]


def _sample_attention(page_table, c31, q, k_new, v_new, mask, b_last, b_new, cache_k, cache_v):
    DB, n_pages = page_table.shape
    G = N_HEADS_A // N_KV_A
    full = lambda a: pl.BlockSpec(a.shape, lambda b, pt, c: (0,) * a.ndim)
    grid_spec = pltpu.PrefetchScalarGridSpec(
        num_scalar_prefetch=2,
        grid=(DB,),
        in_specs=[pl.BlockSpec((1, N_KV_A, G * DEC_T, HEAD_DIM), lambda b, pt, c: (b, 0, 0, 0)),
                  pl.BlockSpec((1, LANE, KV_A), lambda b, pt, c: (b, 0, 0)),
                  pl.BlockSpec((1, LANE, KV_A), lambda b, pt, c: (b, 0, 0)),
                  pl.BlockSpec((1, n_pages + 1, DEC_T, LANE), lambda b, pt, c: (b, 0, 0, 0)),
                  full(b_last), full(b_new),
                  pl.BlockSpec(memory_space=pl.ANY), pl.BlockSpec(memory_space=pl.ANY)],
        out_specs=pl.BlockSpec((1, N_KV_A, G * DEC_T, HEAD_DIM), lambda b, pt, c: (b, 0, 0, 0)),
        scratch_shapes=[pltpu.VMEM((2, _pick(n_pages, S_PG, 1), PAGE_SIZE * N_KV_A, HEAD_DIM), F32),
                        pltpu.VMEM((2, _pick(n_pages, S_PG, 1), PAGE_SIZE * N_KV_A, HEAD_DIM), F32),
                        pltpu.VMEM((N_KV_A, G * DEC_T, 1), F32),
                        pltpu.VMEM((N_KV_A, G * DEC_T, 1), F32),
                        pltpu.VMEM((N_KV_A, G * DEC_T, HEAD_DIM), F32),
                        pltpu.SemaphoreType.DMA((2,)),
                        pltpu.SemaphoreType.DMA((2,))],
    )
    return pl.pallas_call(
        functools.partial(_sattn_body, n_pages=n_pages),
        grid_spec=grid_spec,
        out_shape=jax.ShapeDtypeStruct((DB, N_KV_A, G * DEC_T, HEAD_DIM), F32),
        compiler_params=_cparams(("arbitrary",)),
        name="sample_attention",
    )(page_table, c31, q, k_new, v_new, mask, b_last, b_new, cache_k, cache_v)


def _finish_layer(x, o_a, o_b, w_out, ln1_g, ln1_b, w_router, b_router, w1, b1, w2, b2, ln2_g, ln2_b):
    T = x.shape[0]
    E = w1.shape[0]
    half = o_a.shape[1]
    wa = w_out[:half].astype(BF16)
    wb = w_out[half:].astype(BF16)
    wr = jnp.zeros((D_MODEL, LANE), F32).at[:, :E].set(w_router.astype(F32))
    br = jnp.full((1, LANE), NEG, F32).at[0, :E].set(b_router.astype(F32))
    row2 = lambda a: a.astype(F32).reshape(1, -1)
    h, route, counts = _oproj_router(o_a, o_b, x, wa, wb, row2(ln1_g), row2(ln1_b), wr, br,
                                     tm=_pick(T, 256, SUBLANE))
    tm = MOE_TM
    top_e = route[:, R_EXP:R_EXP + TOP_K].astype(I32)
    rank = route[:, R_RANK:R_RANK + TOP_K].astype(I32)
    cnt = counts[0, :E].astype(I32)
    padded = (cnt + tm - 1) // tm * tm
    pend = jnp.cumsum(padded)
    pstart = pend - padded
    dest = (pstart[top_e] + rank).reshape(-1)
    n_blocks = (T * TOP_K + E * (tm - 1) + tm - 1) // tm + 1
    tok = jnp.arange(T * TOP_K, dtype=I32) // TOP_K
    row_tok = jnp.zeros((n_blocks * tm,), I32).at[dest].set(tok)
    blk_start = jnp.arange(n_blocks, dtype=I32) * tm
    blk_e = jnp.minimum(jnp.sum((pend[None, :] <= blk_start[:, None]).astype(I32), axis=1), E - 1)
    n_used = (pend[-1:] // tm).astype(I32)
    y_rows = _moe_ffn(blk_e, n_used, row_tok, h, w1, b1.astype(F32).reshape(E, 1, -1),
                      w2, b2.astype(F32).reshape(E, 1, -1), tm, MOE_TF)
    return _moe_combine(dest, y_rows, h, route, row2(ln2_g), row2(ln2_b), _pick(T, CMB_TN, SUBLANE))


def _pick(n, cap, mult):
    best = None
    for d in range(mult, min(n, cap) + 1, mult):
        if n % d == 0:
            best = d
    assert best is not None, (n, cap, mult)
    return best


def kernel(x_prompt, x_sample, cache_k, cache_v, cache_kidx, state_conv, state_ssm, page_table, w_in, rel_table,
           w_out, conv_w, A_log, dt_bias, gdn_norm_w, ln1_g, ln1_b, w_router, b_router, w1, b1, w2, b2,
           ln2_g, ln2_b):
    BP, T, D = x_prompt.shape
    DB, TS, _ = x_sample.shape
    n_pool = cache_k.shape[0]
    n_pages = page_table.shape[1]
    past = n_pages * PAGE_SIZE
    assert BP == 1 and D == D_MODEL and TS == DEC_T and T % ATT_TK == 0
    G = N_HEADS_A // N_KV_A
    nq = T // LANE
    att_scale = HEAD_DIM ** -0.5
    idx_scale = IDX_DIM ** -0.5 * N_IDX_HEADS ** -0.5

    x_all = jnp.concatenate([x_prompt.reshape(T, D), x_sample.reshape(DB * TS, D)], axis=0).astype(F32)
    TA = x_all.shape[0]
    proj = _proj(x_all, _arrange_w_in(w_in), _pick(TA, 1280, SUBLANE), 768)
    bias, bias_log2 = _bias_tiles(rel_table)
    c31 = rel_table[N_BUCKETS - 1].astype(F32)

    pp = proj[:T]
    k_p = pp[:, C_KA:C_KA + KV_A]
    v_p = pp[:, C_VA:C_VA + KV_A]
    ki_p = pp[:, C_MISC + M_KI:C_MISC + M_KI + IDX_DIM]
    mask_p = _prompt_indexer(proj, ki_p.astype(BF16), min(TOPK_MAX, T // 4))
    oa_p = _prompt_attention(proj, k_p.astype(BF16), v_p.T.astype(BF16), mask_p, bias_log2)

    ps = proj[T:].reshape(DB, TS, N_PROJ)
    pad_rows = lambda a: jnp.pad(a, ((0, 0), (0, LANE - TS), (0, 0)))
    q_s = (ps[..., C_QA:C_QA + Q_A].reshape(DB, TS, N_KV_A, G, HEAD_DIM) * att_scale)
    q_s = q_s.transpose(0, 2, 3, 1, 4).reshape(DB, N_KV_A, G * TS, HEAD_DIM).astype(BF16)
    k_s = ps[..., C_KA:C_KA + KV_A]
    v_s = ps[..., C_VA:C_VA + KV_A]
    ki_s = ps[..., C_MISC + M_KI:C_MISC + M_KI + IDX_DIM]
    qi_s = ps[..., C_QI:C_QI + QI_W].reshape(DB, TS, N_IDX_HEADS, IDX_DIM)
    qi_s = qi_s.transpose(0, 2, 1, 3).reshape(DB, N_IDX_HEADS * TS, IDX_DIM).astype(BF16)
    w_s = (ps[..., C_MISC + M_WI:C_MISC + M_WI + N_IDX_HEADS] * idx_scale).transpose(0, 2, 1)
    w_s = w_s.reshape(DB, N_IDX_HEADS * TS, 1)
    mask_s = _sample_indexer(page_table, qi_s, w_s, pad_rows(ki_s), cache_kidx.astype(F32),
                             min(TOPK_MAX, (past + TS) // 4))
    to_rows = lambda t: t[:, :, :TS].transpose(0, 2, 1).reshape(N_KV_A, G * TS, LANE)
    o_s = _sample_attention(page_table, c31, q_s, pad_rows(k_s), pad_rows(v_s), mask_s,
                            to_rows(bias[1]), to_rows(bias[0]),
                            cache_k.astype(F32).reshape(n_pool, PAGE_SIZE * N_KV_A, HEAD_DIM),
                            cache_v.astype(F32).reshape(n_pool, PAGE_SIZE * N_KV_A, HEAD_DIM))
    oa_s = o_s.reshape(DB, N_KV_A, G, TS, HEAD_DIM).transpose(0, 3, 1, 2, 4).reshape(DB * TS, Q_A)

    lanes8 = lambda a: jnp.zeros((1, LANE), F32).at[0, M_A:M_A + N_HEADS_B].set(a.astype(F32))
    gdn_args = (conv_w.astype(F32), lanes8(A_log), lanes8(dt_bias), gdn_norm_w.astype(F32).reshape(1, HEAD_DIM))
    ob_p, cb_p, s_p = _gdn(proj, 0, BP, T, jnp.zeros((BP, SUBLANE, CONV_DIM), F32),
                           jnp.zeros((BP, N_HEADS_B, HEAD_DIM, HEAD_DIM), F32), *gdn_args)
    conv0_s = jnp.pad(state_conv.astype(F32), ((0, 0), (SUBLANE - (CONV_W - 1), 0), (0, 0)))
    ob_s, cb_s, s_s = _gdn(proj, T, DB, TS, conv0_s, state_ssm.astype(F32), *gdn_args)

    y_all = _finish_layer(x_all, jnp.concatenate([oa_p, oa_s], axis=0), jnp.concatenate([ob_p, ob_s], axis=0),
                          w_out, ln1_g, ln1_b, w_router, b_router, w1, b1, w2, b2, ln2_g, ln2_b)
    tail = SUBLANE - (CONV_W - 1)
    return (y_all[:T].reshape(BP, T, D), y_all[T:].reshape(DB, TS, D),
            k_p.reshape(BP, T, N_KV_A, HEAD_DIM), v_p.reshape(BP, T, N_KV_A, HEAD_DIM),
            ki_p.reshape(BP, T, IDX_DIM), cb_p[:, tail:], s_p,
            k_s.reshape(DB, TS, N_KV_A, HEAD_DIM), v_s.reshape(DB, TS, N_KV_A, HEAD_DIM), ki_s,
            cb_s[:, tail:], s_s)
```

```python
import functools
import math

import jax
import jax.numpy as jnp
import numpy as np
from jax import lax
from jax.experimental import pallas as pl
from jax.experimental.pallas import tpu as pltpu

F32 = jnp.float32
BF16 = jnp.bfloat16
I32 = jnp.int32

D_MODEL = 2048
HEAD_DIM = 128
N_HEADS_A = 8
N_KV_A = 4
N_IDX_HEADS = 16
IDX_DIM = 64
TOPK_MAX = 256
N_BUCKETS = 32
MAX_DISTANCE = 128
N_HEADS_B = 8
CONV_W = 4
CONV_DIM = N_HEADS_B * 3 * HEAD_DIM
TOP_K = 4
D_FF = D_MODEL
SWIGLU_LIMIT = 7.0
SWIGLU_ALPHA = 1.702
PAGE_SIZE = 128
DEPTH = 1
DEEPNORM_ALPHA = (2 * DEPTH) ** 0.25
LN_EPS = 1e-5

Q_A = N_HEADS_A * HEAD_DIM
KV_A = N_KV_A * HEAD_DIM
QI_W = N_IDX_HEADS * IDX_DIM
Z_B = N_HEADS_B * HEAD_DIM

LANE = 128
SUBLANE = 8
VMEM_LIMIT = 56 * 1024 * 1024

C_QA = 0
C_KA = C_QA + Q_A
C_VA = C_KA + KV_A
C_QI = C_VA + KV_A
C_QKV = C_QI + QI_W
C_Z = C_QKV + CONV_DIM
C_MISC = C_Z + Z_B
M_KI, M_WI, M_A, M_B = 0, 64, 80, 88
N_PROJ = 7680

NEG = -1e30
LOG2E = 1.4426950408889634
GDN_C = 128
GDN_HG = 4


def _cparams(sem, vmem=VMEM_LIMIT):
    return pltpu.CompilerParams(dimension_semantics=sem, vmem_limit_bytes=vmem)


def _proj_body(x_ref, w_ref, o_ref, xb_ref):
    @pl.when(pl.program_id(1) == 0)
    def _():
        xb_ref[...] = x_ref[...].astype(BF16)

    o_ref[...] = jnp.dot(xb_ref[...], w_ref[...], preferred_element_type=F32)


def _proj(x, w, tm, tn):
    T, D = x.shape
    N = w.shape[1]
    return pl.pallas_call(
        _proj_body,
        grid=(T // tm, N // tn),
        in_specs=[pl.BlockSpec((tm, D), lambda i, j: (i, 0)),
                  pl.BlockSpec((D, tn), lambda i, j: (0, j))],
        out_specs=pl.BlockSpec((tm, tn), lambda i, j: (i, j)),
        out_shape=jax.ShapeDtypeStruct((T, N), F32),
        scratch_shapes=[pltpu.VMEM((tm, D), BF16)],
        compiler_params=_cparams(("arbitrary", "arbitrary")),
        name="in_proj",
    )(x, w)


def _arrange_w_in(w_in):
    offs = np.cumsum([0, Q_A, KV_A, KV_A, QI_W, IDX_DIM, N_IDX_HEADS, CONV_DIM, Z_B, N_HEADS_B, N_HEADS_B])
    seg = lambda k: w_in[:, offs[k]:offs[k + 1]]
    qa, ka, va, qi, ki, wi, qkv, z, a, b = (seg(k) for k in range(10))
    pad_misc = jnp.zeros((w_in.shape[0], LANE - (IDX_DIM + N_IDX_HEADS + 2 * N_HEADS_B)), w_in.dtype)
    pad_tail = jnp.zeros((w_in.shape[0], N_PROJ - (C_MISC + LANE)), w_in.dtype)
    return jnp.concatenate([qa, ka, va, qi, qkv, z, ki, wi, a, b, pad_misc, pad_tail], axis=1).astype(BF16)


def _bias_body(rel_ref, o_ref, o2_ref):
    d = pl.program_id(0)
    s = lax.broadcasted_iota(I32, (LANE, LANE), 0)
    t = lax.broadcasted_iota(I32, (LANE, LANE), 1)
    n = jnp.maximum(d * LANE + t - s, 0)
    max_exact = N_BUCKETS // 2
    nf = jnp.maximum(n, 1).astype(F32)
    large = max_exact + (jnp.log(nf / max_exact) / math.log(MAX_DISTANCE / max_exact)
                         * (N_BUCKETS - max_exact)).astype(I32)
    large = jnp.minimum(large, N_BUCKETS - 1)
    bucket = jnp.where(n < max_exact, n, large)
    for h in range(N_HEADS_A):
        acc = jnp.zeros((LANE, LANE), F32)
        for b in range(N_BUCKETS):
            acc = jnp.where(bucket == b, rel_ref[b, h], acc)
        o_ref[0, h] = acc
        o2_ref[0, h] = acc * LOG2E


def _bias_tiles(rel_table):
    spec = pl.BlockSpec((1, N_HEADS_A, LANE, LANE), lambda d: (d, 0, 0, 0))
    shape = jax.ShapeDtypeStruct((3, N_HEADS_A, LANE, LANE), F32)
    return pl.pallas_call(
        _bias_body,
        grid=(3,),
        in_specs=[pl.BlockSpec(memory_space=pltpu.SMEM)],
        out_specs=[spec, spec],
        out_shape=[shape, shape],
        compiler_params=_cparams(("arbitrary",)),
        name="rel_bias_tiles",
    )(rel_table.astype(F32))


INT_MIN = -2 ** 31
NEG_INF_KEY = int(np.int32(np.array(-np.inf, np.float32).view(np.int32)) ^ np.int32(0x7FFFFFFF))


def _sortable(x):
    b = pltpu.bitcast(x, I32)
    return b ^ (lax.shift_right_arithmetic(b, 31) & jnp.int32(0x7FFFFFFF))


def _kth_largest_key(count_ge, shape, k):
    def bit_body(b, c):
        cand = c + lax.shift_left(jnp.int32(1), 31 - b)
        return jnp.where(count_ge(cand) >= k, cand, c)

    return lax.fori_loop(0, 32, bit_body, jnp.full(shape, INT_MIN, I32))


IDX_CH = 512


def _pidx_body(qi_ref, misc_ref, ki_ref, o_ref, keys_ref, qt_ref, w_ref, *, topk, n_keys):
    i = pl.program_id(0)
    nch = (i * LANE + LANE + IDX_CH - 1) // IDX_CH
    t_idx = i * LANE + lax.broadcasted_iota(I32, (IDX_CH, LANE), 1)
    s_loc = lax.broadcasted_iota(I32, (IDX_CH, LANE), 0)

    for p in range(N_IDX_HEADS // 2):
        xt = qi_ref[:, p * LANE:(p + 1) * LANE].T
        qt_ref[p] = jnp.concatenate([xt[:IDX_DIM], xt[IDX_DIM:]], axis=1).astype(BF16)
    w_ref[...] = misc_ref[...].T * (IDX_DIM ** -0.5 * N_IDX_HEADS ** -0.5)

    def score_chunk(c, carry):
        k0 = pl.multiple_of(c * IDX_CH, IDX_CH)
        kc = ki_ref[pl.ds(k0, IDX_CH), :]
        acc = jnp.zeros((IDX_CH, LANE), F32)
        for p in range(N_IDX_HEADS // 2):
            r = jnp.dot(kc, qt_ref[p], preferred_element_type=F32)
            acc = acc + w_ref[M_WI + 2 * p:M_WI + 2 * p + 1, :] * jnp.maximum(r[:, :LANE], 0.0)
            acc = acc + w_ref[M_WI + 2 * p + 1:M_WI + 2 * p + 2, :] * jnp.maximum(r[:, LANE:], 0.0)
        acc = jnp.where(k0 + s_loc <= t_idx, acc, -jnp.inf)
        keys_ref[pl.ds(k0, IDX_CH), :] = _sortable(acc)
        return carry

    lax.fori_loop(0, nch, score_chunk, 0)

    def count(pred):
        def body(c, cnt):
            k0 = pl.multiple_of(c * IDX_CH, IDX_CH)
            m = pred(keys_ref[pl.ds(k0, IDX_CH), :], k0).astype(I32)
            return cnt + jnp.sum(m.reshape(IDX_CH // SUBLANE, SUBLANE, LANE), axis=0)

        cnt = lax.fori_loop(0, nch, body, jnp.zeros((SUBLANE, LANE), I32))
        return jnp.sum(cnt, axis=0, keepdims=True)

    thr = _kth_largest_key(lambda c: count(lambda kb, k0: kb >= c), (1, LANE), topk)
    n_ge = count(lambda kb, k0: kb >= thr)
    n_gt = count(lambda kb, k0: kb > thr)
    need = jnp.where(thr == NEG_INF_KEY, 0, topk - n_gt)
    excess = jnp.max(n_ge - n_gt - need) > 0

    def tie_limit():
        nbits = max(1, int(math.ceil(math.log2(n_keys))))
        def bit_body(b, p):
            cand = p + lax.shift_left(jnp.int32(1), nbits - 1 - b)
            f = count(lambda kb, k0: (kb == thr) & (k0 + s_loc < cand))
            return jnp.where(f < need, cand, p)
        return lax.fori_loop(0, nbits, bit_body, jnp.zeros((1, LANE), I32))

    lim = lax.cond(excess, tie_limit, lambda: jnp.full((1, LANE), n_keys, I32))
    lim = jnp.where(need > 0, lim, -1)

    def write_chunk(c, carry):
        k0 = pl.multiple_of(c * IDX_CH, IDX_CH)
        kb = keys_ref[pl.ds(k0, IDX_CH), :]
        sel = (kb > thr) | ((kb == thr) & (k0 + s_loc <= lim))
        o_ref[pl.ds(k0, IDX_CH), :] = jnp.where(sel, 0.0, NEG).astype(o_ref.dtype)
        return carry

    lax.fori_loop(0, nch, write_chunk, 0)

    def fill_chunk(c, carry):
        k0 = pl.multiple_of(c * IDX_CH, IDX_CH)
        o_ref[pl.ds(k0, IDX_CH), :] = jnp.full((IDX_CH, LANE), NEG, o_ref.dtype)
        return carry

    lax.fori_loop(nch, n_keys // IDX_CH, fill_chunk, 0)


def _prompt_indexer(proj, ki, topk):
    T = ki.shape[0]
    return pl.pallas_call(
        functools.partial(_pidx_body, topk=topk, n_keys=T),
        grid=(T // LANE,),
        in_specs=[pl.BlockSpec((LANE, QI_W), lambda i: (i, C_QI // QI_W)),
                  pl.BlockSpec((LANE, LANE), lambda i: (i, C_MISC // LANE)),
                  pl.BlockSpec((T, IDX_DIM), lambda i: (0, 0))],
        out_specs=pl.BlockSpec((T, LANE), lambda i: (0, i)),
        out_shape=jax.ShapeDtypeStruct((T, T), BF16),
        scratch_shapes=[pltpu.VMEM((T, LANE), I32),
                        pltpu.VMEM((N_IDX_HEADS // 2, IDX_DIM, 2 * LANE), BF16),
                        pltpu.VMEM((LANE, LANE), F32)],
        compiler_params=_cparams(("arbitrary",)),
        name="prompt_indexer",
    )(proj, proj, ki)


ATT_TK = 512


def _pattn_body(ii_ref, jj_ref, q_ref, k_ref, vt_ref, mask_ref, bias_ref, o_ref, qt_ref, m_ref, l_ref, acc_ref,
                lg_ref, pb_ref):
    s = pl.program_id(0)
    i = ii_ref[s]
    j = jj_ref[s]
    G = N_HEADS_A // N_KV_A
    NU = ATT_TK // LANE

    @pl.when(j == 0)
    def _():
        m_ref[...] = jnp.full(m_ref.shape, NEG, F32)
        l_ref[...] = jnp.zeros(l_ref.shape, F32)
        acc_ref[...] = jnp.zeros(acc_ref.shape, F32)
        for h in range(N_HEADS_A):
            qh = q_ref[:, h * HEAD_DIM:(h + 1) * HEAD_DIM] * (HEAD_DIM ** -0.5 * LOG2E)
            qt_ref[h // G, :, (h % G) * LANE:(h % G + 1) * LANE] = qh.T.astype(BF16)

    bidx = [jnp.clip((i * LANE - j * ATT_TK - u * LANE) // LANE, 0, 2) for u in range(NU)]
    m_olds = [m_ref[g] for g in range(N_KV_A)]
    mxs = list(m_olds)
    for u in range(NU):
        rows = slice(u * LANE, (u + 1) * LANE)
        mbu = mask_ref[rows, :].astype(F32)
        for g in range(N_KV_A):
            st = jnp.dot(k_ref[rows, g * HEAD_DIM:(g + 1) * HEAD_DIM], qt_ref[g], preferred_element_type=F32)
            lgu = jnp.concatenate([st[:, r * LANE:(r + 1) * LANE] + (bias_ref[bidx[u], g * G + r] + mbu)
                                   for r in range(G)], axis=1)
            lg_ref[g, rows, :] = lgu
            mxs[g] = jnp.maximum(mxs[g], jnp.max(lgu, axis=0, keepdims=True))
    alphas = [jnp.exp2(m_olds[g] - mxs[g]) for g in range(N_KV_A)]
    lsums = [jnp.zeros_like(mxs[g]) for g in range(N_KV_A)]
    zero = lax.shift_right_arithmetic(j, 31) * LANE
    for u in range(NU):
        rows = slice(u * LANE, (u + 1) * LANE)
        for g in range(N_KV_A):
            p = jnp.exp2(lg_ref[g, pl.ds(pl.multiple_of(zero + u * LANE, LANE), LANE), :] - mxs[g])
            lsums[g] = lsums[g] + jnp.sum(p, axis=0, keepdims=True)
            pb_ref[g, rows, :] = p.astype(BF16)
    for g in range(N_KV_A):
        l_ref[g] = alphas[g] * l_ref[g] + lsums[g]
        m_ref[g] = mxs[g]
        vg = vt_ref[g * HEAD_DIM:(g + 1) * HEAD_DIM, :]
        acc_ref[g] = alphas[g] * acc_ref[g] + jnp.dot(vg, pb_ref[g], preferred_element_type=F32)

    @pl.when((j + 1) * ATT_TK > i * LANE + LANE - 1)
    def _():
        for h in range(N_HEADS_A):
            cs = slice((h % G) * LANE, (h % G + 1) * LANE)
            o_ref[:, h * HEAD_DIM:(h + 1) * HEAD_DIM] = (acc_ref[h // G][:, cs] / l_ref[h // G][:, cs]).T


def _prompt_attention(proj, k, v_t, mask, bias):
    T = k.shape[0]
    nq = T // LANE
    ii, jj = [], []
    for i in range(nq):
        for j in range((i * LANE + LANE - 1) // ATT_TK + 1):
            ii.append(i)
            jj.append(j)
    ii = jnp.asarray(np.array(ii, np.int32))
    jj = jnp.asarray(np.array(jj, np.int32))
    G = N_HEADS_A // N_KV_A
    grid_spec = pltpu.PrefetchScalarGridSpec(
        num_scalar_prefetch=2,
        grid=(int(ii.shape[0]),),
        in_specs=[pl.BlockSpec((LANE, Q_A), lambda s, ii, jj: (ii[s], C_QA // Q_A)),
                  pl.BlockSpec((ATT_TK, KV_A), lambda s, ii, jj: (jj[s], 0)),
                  pl.BlockSpec((KV_A, ATT_TK), lambda s, ii, jj: (0, jj[s])),
                  pl.BlockSpec((ATT_TK, LANE), lambda s, ii, jj: (jj[s], ii[s])),
                  pl.BlockSpec((3, N_HEADS_A, LANE, LANE), lambda s, ii, jj: (0, 0, 0, 0))],
        out_specs=pl.BlockSpec((LANE, Q_A), lambda s, ii, jj: (ii[s], 0)),
        scratch_shapes=[pltpu.VMEM((N_KV_A, HEAD_DIM, G * LANE), BF16),
                        pltpu.VMEM((N_KV_A, 1, G * LANE), F32),
                        pltpu.VMEM((N_KV_A, 1, G * LANE), F32),
                        pltpu.VMEM((N_KV_A, HEAD_DIM, G * LANE), F32),
                        pltpu.VMEM((N_KV_A, ATT_TK, G * LANE), F32),
                        pltpu.VMEM((N_KV_A, ATT_TK, G * LANE), BF16)],
    )
    return pl.pallas_call(
        _pattn_body,
        grid_spec=grid_spec,
        out_shape=jax.ShapeDtypeStruct((T, Q_A), F32),
        compiler_params=_cparams(("arbitrary",)),
        name="prompt_attention",
    )(ii, jj, proj, k, v_t, mask, bias)


def _split3(a):
    a1 = a.astype(BF16)
    r = a - a1.astype(F32)
    a2 = r.astype(BF16)
    a3 = (r - a2.astype(F32)).astype(BF16)
    return a1, a2, a3


def _dot_f32(a, b):
    a1, a2, a3 = _split3(a)
    b1, b2, b3 = _split3(b)
    d = lambda p, q: jnp.dot(p, q, preferred_element_type=F32)
    return (d(a1, b1) + (d(a1, b2) + d(a2, b1))) + ((d(a1, b3) + d(a3, b1)) + d(a2, b2))


def _split2(a):
    a1 = a.astype(BF16)
    return a1, (a - a1.astype(F32)).astype(BF16)


def _dot_3p(a, b):
    a1, a2 = _split2(a)
    b1, b2 = _split2(b)
    d = lambda p, q: jnp.dot(p, q, preferred_element_type=F32)
    return d(a1, b1) + (d(a1, b2) + d(a2, b1))


def _dot_bf(a, b):
    return jnp.dot(a.astype(BF16), b.astype(BF16), preferred_element_type=F32)


def _dot_nt(a, b):
    return lax.dot_general(a.astype(BF16), b.astype(BF16), (((1,), (1,)), ((), ())), preferred_element_type=F32)


def _sigmoid(x):
    return 1.0 / (1.0 + jnp.exp(-x))


def _softplus(x):
    return jnp.maximum(x, 0.0) + jnp.log(1.0 + jnp.exp(-jnp.abs(x)))


def _gdn_body(qkv_ref, z_ref, misc_ref, conv0_ref, s0_ref, cw_ref, arow_ref, dtrow_ref, nw_ref,
              o_ref, convo_ref, so_ref, xbuf_ref, s_ref, *, n_valid):
    C = GDN_C
    n = pl.program_id(1)
    H = N_HEADS_B

    @pl.when(n == 0)
    def _():
        xbuf_ref[0:SUBLANE, :] = conv0_ref[0]
        s_ref[...] = s0_ref[0]
        if n_valid < C:
            xbuf_ref[SUBLANE:, :] = jnp.zeros((C, CONV_DIM), F32)

    xbuf_ref[SUBLANE:SUBLANE + n_valid, :] = qkv_ref[...]
    conv = xbuf_ref[5:5 + C, :] * cw_ref[0:1, :]
    for i in range(1, CONV_W):
        conv = conv + xbuf_ref[5 + i:5 + i + C, :] * cw_ref[i:i + 1, :]
    tail = xbuf_ref[n_valid:n_valid + SUBLANE, :]
    xbuf_ref[0:SUBLANE, :] = tail
    convo_ref[0] = tail

    row = lax.broadcasted_iota(I32, (C, LANE), 0)
    col = lax.broadcasted_iota(I32, (C, LANE), 1)
    live = row < n_valid
    hact = conv * _sigmoid(conv)

    misc = misc_ref[...]
    if n_valid < C:
        misc = jnp.concatenate([misc, jnp.zeros((C - n_valid, LANE), F32)], axis=0)
    g_full = jnp.where(live, -jnp.exp(arow_ref[...]) * _softplus(misc + dtrow_ref[...]), 0.0)
    beta_full = jnp.where(live, _sigmoid(misc), 0.0)
    tril = (row >= col).astype(F32)
    gc_full = _dot_f32(tril, g_full)
    gc_t = gc_full.T
    strict = row > col
    incl = row >= col

    def setup(hd):
        sl = slice(hd * HEAD_DIM, (hd + 1) * HEAD_DIM)
        q = hact[:, sl]
        k = hact[:, H * HEAD_DIM + hd * HEAD_DIM:H * HEAD_DIM + (hd + 1) * HEAD_DIM]
        v = hact[:, 2 * H * HEAD_DIM + hd * HEAD_DIM:2 * H * HEAD_DIM + (hd + 1) * HEAD_DIM]
        q = q * lax.rsqrt(jnp.sum(q * q, -1, keepdims=True) + 1e-6) * HEAD_DIM ** -0.5
        k = k * lax.rsqrt(jnp.sum(k * k, -1, keepdims=True) + 1e-6)
        q = jnp.where(live, q, 0.0)
        k = jnp.where(live, k, 0.0)
        gc_col = jnp.broadcast_to(gc_full[:, M_A + hd:M_A + hd + 1], (C, LANE))
        gc_row = jnp.broadcast_to(gc_t[M_A + hd:M_A + hd + 1, :], (C, LANE))
        gc_last = jnp.broadcast_to(gc_full[C - 1:C, M_A + hd:M_A + hd + 1], (C, LANE))
        beta = jnp.broadcast_to(beta_full[:, M_B + hd:M_B + hd + 1], (C, LANE))
        decay = jnp.where(incl, jnp.exp(jnp.where(incl, gc_col - gc_row, 0.0)), 0.0)
        kk = _dot_nt(k, k)
        qk = _dot_nt(q, k) * decay
        e_gc = jnp.exp(gc_col)
        m = jnp.where(strict, -(beta * decay * kk), 0.0)
        y = jnp.concatenate([v * beta, k * (beta * e_gc)], axis=1)
        return dict(q=q, k=k, qk=qk, e_gc=e_gc, gc_col=gc_col, gc_last=gc_last, m=m, y=y)

    def finish(hd, st):
        sl = slice(hd * HEAD_DIM, (hd + 1) * HEAD_DIM)
        u0 = st["y"][:, :HEAD_DIM]
        w = st["y"][:, HEAD_DIM:]
        s_old = s_ref[hd]
        u = u0 - _dot_bf(w, s_old)
        o = _dot_bf(st["q"] * st["e_gc"], s_old) + _dot_bf(st["qk"], u)
        kd = st["k"] * jnp.exp(st["gc_last"] - st["gc_col"])
        s_ref[hd] = jnp.exp(st["gc_last"]) * s_old + _dot_bf(kd.T, u)
        o = o * lax.rsqrt(jnp.mean(o * o, -1, keepdims=True) + 1e-6) * nw_ref[...]
        zz = z_ref[:, sl]
        o_ref[:, sl] = o[:n_valid] * (zz * _sigmoid(zz))

    nlev = int(math.log2(C))
    for h0 in range(0, H, GDN_HG):
        sts = [setup(hd) for hd in range(h0, h0 + GDN_HG)]
        for lvl in range(nlev):
            for st in sts:
                if lvl + 1 < nlev:
                    mp = _dot_3p(st["m"], jnp.concatenate([st["m"], st["y"]], axis=1))
                    st["y"] = st["y"] + mp[:, C:]
                    st["m"] = mp[:, :C]
                else:
                    st["y"] = st["y"] + _dot_3p(st["m"], st["y"])
        for i, st in enumerate(sts):
            finish(h0 + i, st)

    so_ref[0] = s_ref[...]


def _gdn(proj, row0, B, T, conv0, s0, conv_w, a_row, dt_row, norm_w):
    n_valid = min(T, GDN_C)
    nblk = T // n_valid
    r0 = row0 // n_valid
    rb = lambda b, n: r0 + b * nblk + n
    return pl.pallas_call(
        functools.partial(_gdn_body, n_valid=n_valid),
        grid=(B, nblk),
        in_specs=[pl.BlockSpec((n_valid, CONV_DIM), lambda b, n: (rb(b, n), C_QKV // CONV_DIM)),
                  pl.BlockSpec((n_valid, Z_B), lambda b, n: (rb(b, n), C_Z // Z_B)),
                  pl.BlockSpec((n_valid, LANE), lambda b, n: (rb(b, n), C_MISC // LANE)),
                  pl.BlockSpec((1, SUBLANE, CONV_DIM), lambda b, n: (b, 0, 0)),
                  pl.BlockSpec((1, N_HEADS_B, HEAD_DIM, HEAD_DIM), lambda b, n: (b, 0, 0, 0)),
                  pl.BlockSpec((CONV_W, CONV_DIM), lambda b, n: (0, 0)),
                  pl.BlockSpec((1, LANE), lambda b, n: (0, 0)),
                  pl.BlockSpec((1, LANE), lambda b, n: (0, 0)),
                  pl.BlockSpec((1, HEAD_DIM), lambda b, n: (0, 0))],
        out_specs=[pl.BlockSpec((n_valid, Z_B), lambda b, n: (b * nblk + n, 0)),
                   pl.BlockSpec((1, SUBLANE, CONV_DIM), lambda b, n: (b, 0, 0)),
                   pl.BlockSpec((1, N_HEADS_B, HEAD_DIM, HEAD_DIM), lambda b, n: (b, 0, 0, 0))],
        out_shape=[jax.ShapeDtypeStruct((B * T, Z_B), F32),
                   jax.ShapeDtypeStruct((B, SUBLANE, CONV_DIM), F32),
                   jax.ShapeDtypeStruct((B, N_HEADS_B, HEAD_DIM, HEAD_DIM), F32)],
        scratch_shapes=[pltpu.VMEM((SUBLANE + GDN_C, CONV_DIM), F32),
                        pltpu.VMEM((N_HEADS_B, HEAD_DIM, HEAD_DIM), F32)],
        compiler_params=_cparams(("arbitrary", "arbitrary")),
        name="gdn_mixer",
    )(proj, proj, proj, conv0, s0, conv_w, a_row, dt_row, norm_w)


R_EXP, R_GATE, R_RANK = 0, 4, 8


def _layer_norm(x, g, b):
    mu = jnp.mean(x, -1, keepdims=True)
    xc = x - mu
    var = jnp.mean(xc * xc, -1, keepdims=True)
    return xc * lax.rsqrt(var + LN_EPS) * g + b


def _oproj_body(oa_ref, ob_ref, x_ref, wa_ref, wb_ref, g_ref, b_ref, wr_ref, br_ref,
                h_ref, route_ref, cnt_ref, run_ref):
    i = pl.program_id(0)
    tm = x_ref.shape[0]

    @pl.when(i == 0)
    def _():
        run_ref[...] = jnp.zeros(run_ref.shape, F32)

    mix = _dot_bf(oa_ref[...], wa_ref[...]) + _dot_bf(ob_ref[...], wb_ref[...])
    h = _layer_norm(DEEPNORM_ALPHA * x_ref[...] + mix, g_ref[...], b_ref[...])
    h_ref[...] = h
    logits = _dot_f32(h, wr_ref[...]) + br_ref[...]
    lane = lax.broadcasted_iota(I32, (tm, LANE), 1)
    work = logits
    tops, idxs = [], []
    for _ in range(TOP_K):
        m = jnp.max(work, axis=1, keepdims=True)
        idx = jnp.min(jnp.where(work == m, lane, LANE), axis=1, keepdims=True)
        tops.append(m)
        idxs.append(idx)
        work = jnp.where(lane == idx, -jnp.inf, work)
    es = [jnp.exp(t - tops[0]) for t in tops]
    denom = es[0] + es[1] + es[2] + es[3]
    onehot = jnp.zeros((tm, LANE), F32)
    for idx in idxs:
        onehot = onehot + (lane == idx).astype(F32)
    r = lax.broadcasted_iota(I32, (tm, tm), 0)
    c = lax.broadcasted_iota(I32, (tm, tm), 1)
    before = jnp.dot((r > c).astype(BF16), onehot.astype(BF16), preferred_element_type=F32) + run_ref[...]
    rec = jnp.zeros((tm, LANE), F32)
    for k in range(TOP_K):
        rank = jnp.sum(jnp.where(lane == idxs[k], before, 0.0), axis=1, keepdims=True)
        rec = jnp.where(lane == R_EXP + k, idxs[k].astype(F32), rec)
        rec = jnp.where(lane == R_GATE + k, es[k] / denom, rec)
        rec = jnp.where(lane == R_RANK + k, rank, rec)
    route_ref[...] = rec
    run_ref[...] = run_ref[...] + jnp.sum(onehot, axis=0, keepdims=True)
    cnt_ref[...] = run_ref[...]


def _oproj_router(o_a, o_b, x, wa, wb, g1, b1, wr, br, tm):
    T = x.shape[0]
    row = lambda w: pl.BlockSpec((tm, w), lambda i: (i, 0))
    full = lambda a: pl.BlockSpec(a.shape, lambda i: (0,) * a.ndim)
    return pl.pallas_call(
        _oproj_body,
        grid=(T // tm,),
        in_specs=[row(o_a.shape[1]), row(o_b.shape[1]), row(D_MODEL),
                  full(wa), full(wb), full(g1), full(b1), full(wr), full(br)],
        out_specs=[row(D_MODEL), row(LANE), pl.BlockSpec((1, LANE), lambda i: (0, 0))],
        out_shape=[jax.ShapeDtypeStruct((T, D_MODEL), F32),
                   jax.ShapeDtypeStruct((T, LANE), F32),
                   jax.ShapeDtypeStruct((1, LANE), F32)],
        scratch_shapes=[pltpu.VMEM((1, LANE), F32)],
        compiler_params=_cparams(("arbitrary",)),
        name="out_proj_router",
    )(o_a, o_b, x, wa, wb, g1, b1, wr, br)


MOE_TM = 512
MOE_TF = 512
CMB_TN = 256


def _row_copies(tok_ref, h_hbm, xbuf_ref, sem, blk, slot, wait):
    tm = xbuf_ref.shape[1]
    base = blk * tm

    def body(r, carry):
        t = 0 if wait else tok_ref[base + r]
        cp = pltpu.make_async_copy(h_hbm.at[pl.ds(t, 1)], xbuf_ref.at[slot, pl.ds(r, 1)], sem.at[slot])
        cp.wait() if wait else cp.start()
        return carry

    lax.fori_loop(0, tm, body, 0, unroll=8)


def _ffn_body(be_ref, nu_ref, tok_ref, h_hbm, w1g_ref, w1l_ref, b1g_ref, b1l_ref, w2_ref, b2_ref, y_ref,
              xbuf_ref, xb_ref, sem):
    b = pl.program_id(0)
    f = pl.program_id(1)
    nu = nu_ref[0]
    first = f == 0
    tm = xbuf_ref.shape[1]
    tq = tm // (D_FF // w1g_ref.shape[2])

    @pl.when(first & (b == 0))
    def _():
        _row_copies(tok_ref, h_hbm, xbuf_ref, sem, 0, 0, False)

    @pl.when(first & (b < nu))
    def _():
        _row_copies(tok_ref, h_hbm, xbuf_ref, sem, b, b % 2, True)
        xb_ref[...] = xbuf_ref[b % 2].astype(BF16)

    @pl.when(first & (b == nu))
    def _():
        _row_copies(tok_ref, h_hbm, xbuf_ref, sem, 0, (nu % 2), True)

    @pl.when(b < nu)
    def _():
        for r in range(tq):
            row = f * tq + r
            pltpu.make_async_copy(h_hbm.at[pl.ds(tok_ref[(b + 1) * tm + row], 1)],
                                  xbuf_ref.at[(b + 1) % 2, pl.ds(row, 1)], sem.at[(b + 1) % 2]).start()
        x = xb_ref[...]
        hg = jnp.dot(x, w1g_ref[0].astype(BF16), preferred_element_type=F32) + b1g_ref[0]
        hl = jnp.dot(x, w1l_ref[0].astype(BF16), preferred_element_type=F32) + b1l_ref[0]
        glu = jnp.minimum(hg, SWIGLU_LIMIT)
        lin = jnp.clip(hl, -SWIGLU_LIMIT, SWIGLU_LIMIT)
        act = glu * _sigmoid(SWIGLU_ALPHA * glu) * (lin + 1.0)
        part = jnp.dot(act.astype(BF16), w2_ref[0].astype(BF16), preferred_element_type=F32)

        @pl.when(f == 0)
        def _():
            y_ref[...] = part + b2_ref[0]

        @pl.when(f > 0)
        def _():
            y_ref[...] = y_ref[...] + part

    @pl.when((b >= nu_ref[0]) & (f == 0))
    def _():
        y_ref[...] = jnp.zeros(y_ref.shape, F32)


def _moe_ffn(blk_e, n_used, row_tok, h, w1, b1, w2, b2, tm, tf):
    n_rows = row_tok.shape[0]
    D = h.shape[1]
    nf = D_FF // tf
    bb = lambda b, nu: jnp.minimum(b, nu[0] - 1)
    ff = lambda b, f, nu: jnp.where(b < nu[0], f, nf - 1)
    grid_spec = pltpu.PrefetchScalarGridSpec(
        num_scalar_prefetch=3,
        grid=(n_rows // tm, nf),
        in_specs=[pl.BlockSpec(memory_space=pl.ANY),
                  pl.BlockSpec((1, D, tf), lambda b, f, be, nu, tok: (be[bb(b, nu)], 0, ff(b, f, nu))),
                  pl.BlockSpec((1, D, tf), lambda b, f, be, nu, tok: (be[bb(b, nu)], 0, nf + ff(b, f, nu))),
                  pl.BlockSpec((1, 1, tf), lambda b, f, be, nu, tok: (be[bb(b, nu)], 0, ff(b, f, nu))),
                  pl.BlockSpec((1, 1, tf), lambda b, f, be, nu, tok: (be[bb(b, nu)], 0, nf + ff(b, f, nu))),
                  pl.BlockSpec((1, tf, D), lambda b, f, be, nu, tok: (be[bb(b, nu)], ff(b, f, nu), 0)),
                  pl.BlockSpec((1, 1, D), lambda b, f, be, nu, tok: (be[bb(b, nu)], 0, 0))],
        out_specs=pl.BlockSpec((tm, D), lambda b, f, be, nu, tok: (b, 0)),
        scratch_shapes=[pltpu.VMEM((2, tm, D), F32),
                        pltpu.VMEM((tm, D), BF16),
                        pltpu.SemaphoreType.DMA((2,))],
    )
    return pl.pallas_call(
        _ffn_body,
        grid_spec=grid_spec,
        out_shape=jax.ShapeDtypeStruct((n_rows, D), F32),
        compiler_params=_cparams(("arbitrary", "arbitrary")),
        name="moe_ffn",
    )(blk_e, n_used, row_tok, h, w1, w1, b1, b1, w2, b2)


def _combine_body(dest_ref, y_hbm, h_ref, route_ref, g_ref, b_ref, o_ref, buf_ref, sem):
    i = pl.program_id(0)
    tn = h_ref.shape[0]

    def issue(t, carry):
        for k in range(TOP_K):
            d = dest_ref[(i * tn + t) * TOP_K + k]
            pltpu.make_async_copy(y_hbm.at[pl.ds(d, 1)], buf_ref.at[k, pl.ds(t, 1)], sem).start()
        return carry

    lax.fori_loop(0, tn, issue, 0)

    def drain(t, carry):
        for k in range(TOP_K):
            pltpu.make_async_copy(y_hbm.at[pl.ds(0, 1)], buf_ref.at[k, pl.ds(t, 1)], sem).wait()
        return carry

    lax.fori_loop(0, tn, drain, 0)
    route = route_ref[...]
    f = route[:, R_GATE:R_GATE + 1] * buf_ref[0]
    for k in range(1, TOP_K):
        f = f + route[:, R_GATE + k:R_GATE + k + 1] * buf_ref[k]
    o_ref[...] = _layer_norm(DEEPNORM_ALPHA * h_ref[...] + f, g_ref[...], b_ref[...])


def _moe_combine(dest, y_rows, h, route, g2, b2, tn):
    T, D = h.shape
    grid_spec = pltpu.PrefetchScalarGridSpec(
        num_scalar_prefetch=1,
        grid=(T // tn,),
        in_specs=[pl.BlockSpec(memory_space=pl.ANY),
                  pl.BlockSpec((tn, D), lambda i, d: (i, 0)),
                  pl.BlockSpec((tn, LANE), lambda i, d: (i, 0)),
                  pl.BlockSpec((1, D), lambda i, d: (0, 0)),
                  pl.BlockSpec((1, D), lambda i, d: (0, 0))],
        out_specs=pl.BlockSpec((tn, D), lambda i, d: (i, 0)),
        scratch_shapes=[pltpu.VMEM((TOP_K, tn, D), F32), pltpu.SemaphoreType.DMA(())],
    )
    return pl.pallas_call(
        _combine_body,
        grid_spec=grid_spec,
        out_shape=jax.ShapeDtypeStruct((T, D), F32),
        compiler_params=_cparams(("arbitrary",)),
        name="moe_combine",
    )(dest, y_rows, h, route, g2, b2)


S_PG = 16
DEC_T = 8


def _page_copies(cache_hbm, buf_ref, sem, pt_ref, b, gi, slot, wait):
    spg = buf_ref.shape[1]
    for p in range(spg):
        src = cache_hbm.at[0] if wait else cache_hbm.at[pt_ref[b, gi * spg + p]]
        cp = pltpu.make_async_copy(src, buf_ref.at[slot, p], sem.at[slot])
        cp.wait() if wait else cp.start()


def _sidx_body(pt_ref, q_ref, w_ref, knew_ref, cache_hbm, o_ref, buf_ref, keys_ref, sem, *, topk, n_pages):
    b = pl.program_id(0)
    spg = buf_ref.shape[1]
    n_groups = n_pages // spg
    ncp = keys_ref.shape[0]
    q = q_ref[0]
    wl = jnp.broadcast_to(w_ref[0], (N_IDX_HEADS * DEC_T, LANE))

    def scores(kp):
        r = _dot_nt(q, kp)
        r = wl * jnp.maximum(r, 0.0)
        acc = r[0:DEC_T]
        for h in range(1, N_IDX_HEADS):
            acc = acc + r[h * DEC_T:(h + 1) * DEC_T]
        return acc

    _page_copies(cache_hbm, buf_ref, sem, pt_ref, b, 0, 0, False)

    def group(gi, carry):
        slot = gi % 2

        @pl.when(gi + 1 < n_groups)
        def _():
            _page_copies(cache_hbm, buf_ref, sem, pt_ref, b, gi + 1, 1 - slot, False)

        _page_copies(cache_hbm, buf_ref, sem, pt_ref, b, gi, slot, True)
        for p in range(spg):
            keys_ref[gi * spg + p] = _sortable(scores(buf_ref[slot, p]))
        return carry

    lax.fori_loop(0, n_groups, group, 0)
    row = lax.broadcasted_iota(I32, (DEC_T, LANE), 0)
    lane = lax.broadcasted_iota(I32, (DEC_T, LANE), 1)
    keys_ref[n_pages] = _sortable(jnp.where(lane <= row, scores(knew_ref[0]), -jnp.inf))
    for c in range(n_pages + 1, ncp):
        keys_ref[c] = jnp.full((DEC_T, LANE), INT_MIN, I32)

    U = 8

    def count(pred):
        cnt = jnp.zeros((DEC_T, LANE), I32)
        for c0 in range(0, ncp, U):
            kb = keys_ref[c0:c0 + U]
            pos = (c0 + lax.broadcasted_iota(I32, (U, DEC_T, LANE), 0)) * LANE \
                + lax.broadcasted_iota(I32, (U, DEC_T, LANE), 2)
            cnt = cnt + jnp.sum(pred(kb, pos).astype(I32), axis=0)
        return jnp.sum(cnt, axis=1, keepdims=True)

    thr = _kth_largest_key(lambda c: count(lambda kb, pos: kb >= c), (DEC_T, 1), topk)
    n_ge = count(lambda kb, pos: kb >= thr)
    n_gt = count(lambda kb, pos: kb > thr)
    need = jnp.where(thr == NEG_INF_KEY, 0, topk - n_gt)
    excess = jnp.max(n_ge - n_gt - need) > 0
    n_keys = (n_pages + 1) * LANE

    def tie_limit():
        nbits = max(1, int(math.ceil(math.log2(n_keys))))
        def bit_body(bi, p):
            cand = p + lax.shift_left(jnp.int32(1), nbits - 1 - bi)
            f = count(lambda kb, pos: (kb == thr) & (pos < cand))
            return jnp.where(f < need, cand, p)
        return lax.fori_loop(0, nbits, bit_body, jnp.zeros((DEC_T, 1), I32))

    lim = lax.cond(excess, tie_limit, lambda: jnp.full((DEC_T, 1), n_keys, I32))
    lim = jnp.where(need > 0, lim, -1)

    def write(c, carry):
        kb = keys_ref[c]
        pos = c * LANE + lane
        sel = (kb > thr) | ((kb == thr) & (pos <= lim))
        o_ref[0, c] = jnp.where(sel, 0.0, NEG)
        return carry

    lax.fori_loop(0, n_pages + 1, write, 0)


def _sample_indexer(page_table, q, w, k_new, cache_kidx, topk):
    DB, n_pages = page_table.shape
    ncp = (n_pages + 1 + 7) // 8 * 8
    grid_spec = pltpu.PrefetchScalarGridSpec(
        num_scalar_prefetch=1,
        grid=(DB,),
        in_specs=[pl.BlockSpec((1, N_IDX_HEADS * DEC_T, IDX_DIM), lambda b, pt: (b, 0, 0)),
                  pl.BlockSpec((1, N_IDX_HEADS * DEC_T, 1), lambda b, pt: (b, 0, 0)),
                  pl.BlockSpec((1, LANE, IDX_DIM), lambda b, pt: (b, 0, 0)),
                  pl.BlockSpec(memory_space=pl.ANY)],
        out_specs=pl.BlockSpec((1, n_pages + 1, DEC_T, LANE), lambda b, pt: (b, 0, 0, 0)),
        scratch_shapes=[pltpu.VMEM((2, _pick(n_pages, S_PG, 1), PAGE_SIZE, IDX_DIM), F32),
                        pltpu.VMEM((ncp, DEC_T, LANE), I32),
                        pltpu.SemaphoreType.DMA((2,))],
    )
    return pl.pallas_call(
        functools.partial(_sidx_body, topk=topk, n_pages=n_pages),
        grid_spec=grid_spec,
        out_shape=jax.ShapeDtypeStruct((DB, n_pages + 1, DEC_T, LANE), F32),
        compiler_params=_cparams(("arbitrary",)),
        name="sample_indexer",
    )(page_table, q, w, k_new, cache_kidx)


def _sattn_body(pt_ref, c31_ref, q_ref, knew_ref, vnew_ref, mask_ref, blast_ref, bnew_ref, ck_hbm, cv_hbm,
                o_ref, kbuf_ref, vbuf_ref, m_ref, l_ref, acc_ref, sem_k, sem_v, *, n_pages):
    b = pl.program_id(0)
    spg = kbuf_ref.shape[1]
    n_groups = n_pages // spg
    G = N_HEADS_A // N_KV_A
    m_ref[...] = jnp.full(m_ref.shape, NEG, F32)
    l_ref[...] = jnp.zeros(l_ref.shape, F32)
    acc_ref[...] = jnp.zeros(acc_ref.shape, F32)

    def flash(g, lg, vals):
        m_old = m_ref[g]
        m_new = jnp.maximum(m_old, jnp.max(lg, axis=1, keepdims=True))
        alpha = jnp.exp(m_old - m_new)
        p = jnp.exp(lg - m_new)
        l_ref[g] = alpha * l_ref[g] + jnp.sum(p, axis=1, keepdims=True)
        m_ref[g] = m_new
        acc_ref[g] = alpha * acc_ref[g] + _dot_bf(p, vals)

    def const_bias(g):
        rows = lax.broadcasted_iota(I32, (G * DEC_T, LANE), 0)
        out = jnp.full((G * DEC_T, LANE), c31_ref[g * G], F32)
        for r in range(1, G):
            out = jnp.where(rows >= r * DEC_T, c31_ref[g * G + r], out)
        return out

    _page_copies(ck_hbm, kbuf_ref, sem_k, pt_ref, b, 0, 0, False)
    _page_copies(cv_hbm, vbuf_ref, sem_v, pt_ref, b, 0, 0, False)

    def group(gi, carry):
        slot = gi % 2

        @pl.when(gi + 1 < n_groups)
        def _():
            _page_copies(ck_hbm, kbuf_ref, sem_k, pt_ref, b, gi + 1, 1 - slot, False)
            _page_copies(cv_hbm, vbuf_ref, sem_v, pt_ref, b, gi + 1, 1 - slot, False)

        _page_copies(ck_hbm, kbuf_ref, sem_k, pt_ref, b, gi, slot, True)
        _page_copies(cv_hbm, vbuf_ref, sem_v, pt_ref, b, gi, slot, True)
        mb = jnp.concatenate([mask_ref[0, gi * spg + p] for p in range(spg)], axis=1)
        mb = jnp.concatenate([mb] * G, axis=0)
        for g in range(N_KV_A):
            sl = slice(g * HEAD_DIM, (g + 1) * HEAD_DIM)
            cb = const_bias(g)
            last_b = jnp.where(gi == n_groups - 1, blast_ref[g], cb)
            head = pl.ds(g, PAGE_SIZE, stride=N_KV_A)
            parts = [_dot_nt(q_ref[0, g], kbuf_ref[slot, p, head, :]) + (cb if p + 1 < spg else last_b)
                     for p in range(spg)]
            vals = jnp.concatenate([vbuf_ref[slot, p, head, :] for p in range(spg)], axis=0)
            flash(g, jnp.concatenate(parts, axis=1) + mb, vals)
        return carry

    lax.fori_loop(0, n_groups, group, 0)
    mb_new = jnp.concatenate([mask_ref[0, n_pages]] * G, axis=0)
    for g in range(N_KV_A):
        sl = slice(g * HEAD_DIM, (g + 1) * HEAD_DIM)
        flash(g, _dot_nt(q_ref[0, g], knew_ref[0, :, sl]) + bnew_ref[g] + mb_new, vnew_ref[0, :, sl])
    for g in range(N_KV_A):
        o_ref[0, g] = acc_ref[g] / l_ref[g]


def _sample_attention(page_table, c31, q, k_new, v_new, mask, b_last, b_new, cache_k, cache_v):
    DB, n_pages = page_table.shape
    G = N_HEADS_A // N_KV_A
    full = lambda a: pl.BlockSpec(a.shape, lambda b, pt, c: (0,) * a.ndim)
    grid_spec = pltpu.PrefetchScalarGridSpec(
        num_scalar_prefetch=2,
        grid=(DB,),
        in_specs=[pl.BlockSpec((1, N_KV_A, G * DEC_T, HEAD_DIM), lambda b, pt, c: (b, 0, 0, 0)),
                  pl.BlockSpec((1, LANE, KV_A), lambda b, pt, c: (b, 0, 0)),
                  pl.BlockSpec((1, LANE, KV_A), lambda b, pt, c: (b, 0, 0)),
                  pl.BlockSpec((1, n_pages + 1, DEC_T, LANE), lambda b, pt, c: (b, 0, 0, 0)),
                  full(b_last), full(b_new),
                  pl.BlockSpec(memory_space=pl.ANY), pl.BlockSpec(memory_space=pl.ANY)],
        out_specs=pl.BlockSpec((1, N_KV_A, G * DEC_T, HEAD_DIM), lambda b, pt, c: (b, 0, 0, 0)),
        scratch_shapes=[pltpu.VMEM((2, _pick(n_pages, S_PG, 1), PAGE_SIZE * N_KV_A, HEAD_DIM), F32),
                        pltpu.VMEM((2, _pick(n_pages, S_PG, 1), PAGE_SIZE * N_KV_A, HEAD_DIM), F32),
                        pltpu.VMEM((N_KV_A, G * DEC_T, 1), F32),
                        pltpu.VMEM((N_KV_A, G * DEC_T, 1), F32),
                        pltpu.VMEM((N_KV_A, G * DEC_T, HEAD_DIM), F32),
                        pltpu.SemaphoreType.DMA((2,)),
                        pltpu.SemaphoreType.DMA((2,))],
    )
    return pl.pallas_call(
        functools.partial(_sattn_body, n_pages=n_pages),
        grid_spec=grid_spec,
        out_shape=jax.ShapeDtypeStruct((DB, N_KV_A, G * DEC_T, HEAD_DIM), F32),
        compiler_params=_cparams(("arbitrary",)),
        name="sample_attention",
    )(page_table, c31, q, k_new, v_new, mask, b_last, b_new, cache_k, cache_v)


def _finish_layer(x, o_a, o_b, w_out, ln1_g, ln1_b, w_router, b_router, w1, b1, w2, b2, ln2_g, ln2_b):
    T = x.shape[0]
    E = w1.shape[0]
    half = o_a.shape[1]
    wa = w_out[:half].astype(BF16)
    wb = w_out[half:].astype(BF16)
    wr = jnp.zeros((D_MODEL, LANE), F32).at[:, :E].set(w_router.astype(F32))
    br = jnp.full((1, LANE), NEG, F32).at[0, :E].set(b_router.astype(F32))
    row2 = lambda a: a.astype(F32).reshape(1, -1)
    h, route, counts = _oproj_router(o_a, o_b, x, wa, wb, row2(ln1_g), row2(ln1_b), wr, br,
                                     tm=_pick(T, 256, SUBLANE))
    tm = MOE_TM
    top_e = route[:, R_EXP:R_EXP + TOP_K].astype(I32)
    rank = route[:, R_RANK:R_RANK + TOP_K].astype(I32)
    cnt = counts[0, :E].astype(I32)
    padded = (cnt + tm - 1) // tm * tm
    pend = jnp.cumsum(padded)
    pstart = pend - padded
    dest = (pstart[top_e] + rank).reshape(-1)
    n_blocks = (T * TOP_K + E * (tm - 1) + tm - 1) // tm + 1
    tok = jnp.arange(T * TOP_K, dtype=I32) // TOP_K
    row_tok = jnp.zeros((n_blocks * tm,), I32).at[dest].set(tok)
    blk_start = jnp.arange(n_blocks, dtype=I32) * tm
    blk_e = jnp.minimum(jnp.sum((pend[None, :] <= blk_start[:, None]).astype(I32), axis=1), E - 1)
    n_used = (pend[-1:] // tm).astype(I32)
    y_rows = _moe_ffn(blk_e, n_used, row_tok, h, w1, b1.astype(F32).reshape(E, 1, -1),
                      w2, b2.astype(F32).reshape(E, 1, -1), tm, MOE_TF)
    return _moe_combine(dest, y_rows, h, route, row2(ln2_g), row2(ln2_b), _pick(T, CMB_TN, SUBLANE))


def _pick(n, cap, mult):
    best = None
    for d in range(mult, min(n, cap) + 1, mult):
        if n % d == 0:
            best = d
    assert best is not None, (n, cap, mult)
    return best


def kernel(x_prompt, x_sample, cache_k, cache_v, cache_kidx, state_conv, state_ssm, page_table, w_in, rel_table,
           w_out, conv_w, A_log, dt_bias, gdn_norm_w, ln1_g, ln1_b, w_router, b_router, w1, b1, w2, b2,
           ln2_g, ln2_b):
    BP, T, D = x_prompt.shape
    DB, TS, _ = x_sample.shape
    n_pool = cache_k.shape[0]
    n_pages = page_table.shape[1]
    past = n_pages * PAGE_SIZE
    assert BP == 1 and D == D_MODEL and TS == DEC_T and T % ATT_TK == 0
    G = N_HEADS_A // N_KV_A
    nq = T // LANE
    att_scale = HEAD_DIM ** -0.5
    idx_scale = IDX_DIM ** -0.5 * N_IDX_HEADS ** -0.5

    x_all = jnp.concatenate([x_prompt.reshape(T, D), x_sample.reshape(DB * TS, D)], axis=0).astype(F32)
    TA = x_all.shape[0]
    proj = _proj(x_all, _arrange_w_in(w_in), _pick(TA, 1280, SUBLANE), 768)
    bias, bias_log2 = _bias_tiles(rel_table)
    c31 = rel_table[N_BUCKETS - 1].astype(F32)

    pp = proj[:T]
    k_p = pp[:, C_KA:C_KA + KV_A]
    v_p = pp[:, C_VA:C_VA + KV_A]
    ki_p = pp[:, C_MISC + M_KI:C_MISC + M_KI + IDX_DIM]
    mask_p = _prompt_indexer(proj, ki_p.astype(BF16), min(TOPK_MAX, T // 4))
    oa_p = _prompt_attention(proj, k_p.astype(BF16), v_p.T.astype(BF16), mask_p, bias_log2)

    ps = proj[T:].reshape(DB, TS, N_PROJ)
    pad_rows = lambda a: jnp.pad(a, ((0, 0), (0, LANE - TS), (0, 0)))
    q_s = (ps[..., C_QA:C_QA + Q_A].reshape(DB, TS, N_KV_A, G, HEAD_DIM) * att_scale)
    q_s = q_s.transpose(0, 2, 3, 1, 4).reshape(DB, N_KV_A, G * TS, HEAD_DIM).astype(BF16)
    k_s = ps[..., C_KA:C_KA + KV_A]
    v_s = ps[..., C_VA:C_VA + KV_A]
    ki_s = ps[..., C_MISC + M_KI:C_MISC + M_KI + IDX_DIM]
    qi_s = ps[..., C_QI:C_QI + QI_W].reshape(DB, TS, N_IDX_HEADS, IDX_DIM)
    qi_s = qi_s.transpose(0, 2, 1, 3).reshape(DB, N_IDX_HEADS * TS, IDX_DIM).astype(BF16)
    w_s = (ps[..., C_MISC + M_WI:C_MISC + M_WI + N_IDX_HEADS] * idx_scale).transpose(0, 2, 1)
    w_s = w_s.reshape(DB, N_IDX_HEADS * TS, 1)
    mask_s = _sample_indexer(page_table, qi_s, w_s, pad_rows(ki_s), cache_kidx.astype(F32),
                             min(TOPK_MAX, (past + TS) // 4))
    to_rows = lambda t: t[:, :, :TS].transpose(0, 2, 1).reshape(N_KV_A, G * TS, LANE)
    o_s = _sample_attention(page_table, c31, q_s, pad_rows(k_s), pad_rows(v_s), mask_s,
                            to_rows(bias[1]), to_rows(bias[0]),
                            cache_k.astype(F32).reshape(n_pool, PAGE_SIZE * N_KV_A, HEAD_DIM),
                            cache_v.astype(F32).reshape(n_pool, PAGE_SIZE * N_KV_A, HEAD_DIM))
    oa_s = o_s.reshape(DB, N_KV_A, G, TS, HEAD_DIM).transpose(0, 3, 1, 2, 4).reshape(DB * TS, Q_A)

    lanes8 = lambda a: jnp.zeros((1, LANE), F32).at[0, M_A:M_A + N_HEADS_B].set(a.astype(F32))
    gdn_args = (conv_w.astype(F32), lanes8(A_log), lanes8(dt_bias), gdn_norm_w.astype(F32).reshape(1, HEAD_DIM))
    ob_p, cb_p, s_p = _gdn(proj, 0, BP, T, jnp.zeros((BP, SUBLANE, CONV_DIM), F32),
                           jnp.zeros((BP, N_HEADS_B, HEAD_DIM, HEAD_DIM), F32), *gdn_args)
    conv0_s = jnp.pad(state_conv.astype(F32), ((0, 0), (SUBLANE - (CONV_W - 1), 0), (0, 0)))
    ob_s, cb_s, s_s = _gdn(proj, T, DB, TS, conv0_s, state_ssm.astype(F32), *gdn_args)

    y_all = _finish_layer(x_all, jnp.concatenate([oa_p, oa_s], axis=0), jnp.concatenate([ob_p, ob_s], axis=0),
                          w_out, ln1_g, ln1_b, w_router, b_router, w1, b1, w2, b2, ln2_g, ln2_b)
    tail = SUBLANE - (CONV_W - 1)
    return (y_all[:T].reshape(BP, T, D), y_all[T:].reshape(DB, TS, D),
            k_p.reshape(BP, T, N_KV_A, HEAD_DIM), v_p.reshape(BP, T, N_KV_A, HEAD_DIM),
            ki_p.reshape(BP, T, IDX_DIM), cb_p[:, tail:], s_p,
            k_s.reshape(DB, TS, N_KV_A, HEAD_DIM), v_s.reshape(DB, TS, N_KV_A, HEAD_DIM), ki_s,
            cb_s[:, tail:], s_s)
```
